```python
import jax, jax.numpy as jnp
from jax import lax
import numpy as np

D_MODEL = 1024
BATCH = 2
SEQ = 16384
DEPTH = 1
DEC_BATCH = 16
DEC_SEQ = 64
PAST_LEN = 4096

CHUNK = 64
D_PLE = 256
D_FF = 2816
CONV_CH = 1024
CONV_WIDTH = 31
SSD_HEADS = 16
SSD_HEAD_DIM = 64
SSD_INNER = SSD_HEADS * SSD_HEAD_DIM
SSD_GROUPS = 2
SSD_STATE = 128
SSD_CONV_WIDTH = 4
SSD_CONV_DIM = SSD_INNER + 2 * SSD_GROUPS * SSD_STATE
MIX_WIDTH = CONV_CH + SSD_INNER
IN_PROJ = 2 * CONV_CH + SSD_INNER + SSD_CONV_DIM + SSD_HEADS
EPS = 1e-6

kernel_name = "hybrid_conformer_ssd_streaming_step"


def rms_norm(x, g):
    xf = x.astype(jnp.float32)
    y = xf * lax.rsqrt(jnp.mean(xf * xf, axis=-1, keepdims=True) + EPS)
    return (y * g.astype(jnp.float32)).astype(x.dtype)


def layer_norm(x, g, b):
    xf = x.astype(jnp.float32)
    mu = jnp.mean(xf, axis=-1, keepdims=True)
    xc = xf - mu
    y = xc * lax.rsqrt(jnp.mean(xc * xc, axis=-1, keepdims=True) + EPS)
    return (y * g.astype(jnp.float32) + b.astype(jnp.float32)).astype(x.dtype)


def swiglu_ffn(x, w_gate, w_up, w_down):
    return (jax.nn.silu(x @ w_gate) * (x @ w_up)) @ w_down


def causal_dwconv(x_pad, w, b):
    c = x_pad.shape[-1]
    y = lax.conv_general_dilated(x_pad, w[:, None, :].astype(x_pad.dtype), (1,), 'VALID',
                                 dimension_numbers=('NWC', 'WIO', 'NWC'),
                                 feature_group_count=c)
    return y + b.astype(x_pad.dtype)


def ssd_chunk_step(S, inp, A):
    x, dt, B, C = inp
    L = x.shape[1]
    acum = jnp.cumsum(dt * A, axis=1)
    causal = jnp.tril(jnp.ones((L, L), dtype=bool))[None, :, :, None, None]
    diff = acum[:, :, None] - acum[:, None, :]
    decay = jnp.where(causal, jnp.exp(jnp.where(causal, diff, 0.0)), 0.0)
    cb = jnp.einsum('btgn,bsgn->btsg', C, B)
    w = cb[..., None] * decay * dt[:, None]
    y = jnp.einsum('btsgj,bsgjp->btgjp', w, x)
    y = y + jnp.einsum('btgn,bgjpn->btgjp', C, S) * jnp.exp(acum)[..., None]
    w_end = jnp.exp(acum[:, -1:] - acum) * dt
    S_new = (S * jnp.exp(acum[:, -1])[..., None, None]
             + jnp.einsum('bsgj,bsgn,bsgjp->bgjpn', w_end, B, x))
    return S_new, y


def ssd_scan(x, dt, A, B, C, S0):
    b, T = x.shape[0], x.shape[1]
    L = min(CHUNK, T)
    nc = T // L

    def to_blocks(a):
        return jnp.moveaxis(a.reshape((b, nc, L) + a.shape[2:]), 1, 0)

    S, ys = lax.scan(lambda s, inp: ssd_chunk_step(s, inp, A), S0,
                     (to_blocks(x), to_blocks(dt), to_blocks(B), to_blocks(C)))
    y = jnp.moveaxis(ys, 0, 1).reshape(x.shape)
    return y, S


def hybrid_mixer(u, conv_buf, xbc_buf, ssm, conv_dw_w, conv_dw_b, conv_ln_g, conv_ln_b,
                 ssd_conv_w, ssd_conv_b, ssd_dt_bias, ssd_A_log, ssd_D, ssd_norm):
    b, T, _ = u.shape
    f32 = jnp.float32
    G, J, P, N = SSD_GROUPS, SSD_HEADS // SSD_GROUPS, SSD_HEAD_DIM, SSD_STATE
    o1 = CONV_CH
    o2 = 2 * CONV_CH
    o3 = o2 + SSD_INNER
    o4 = o3 + SSD_CONV_DIM
    c_val, c_gate, z, xbc, dt_raw = u[..., :o1], u[..., o1:o2], u[..., o2:o3], u[..., o3:o4], u[..., o4:]

    a = c_val * jax.nn.sigmoid(c_gate)
    a_pad = jnp.concatenate([conv_buf.astype(a.dtype), a], axis=1)
    new_conv_buf = a_pad[:, -(CONV_WIDTH - 1):]
    c = jax.nn.silu(layer_norm(causal_dwconv(a_pad, conv_dw_w, conv_dw_b), conv_ln_g, conv_ln_b))

    xbc_pad = jnp.concatenate([xbc_buf.astype(xbc.dtype), xbc], axis=1)
    new_xbc_buf = xbc_pad[:, -(SSD_CONV_WIDTH - 1):]
    xbc = jax.nn.silu(causal_dwconv(xbc_pad, ssd_conv_w, ssd_conv_b))
    xs = xbc[..., :SSD_INNER]
    Bm = xbc[..., SSD_INNER:SSD_INNER + G * N].astype(f32).reshape(b, T, G, N)
    Cm = xbc[..., SSD_INNER + G * N:].astype(f32).reshape(b, T, G, N)
    dt = jax.nn.softplus(dt_raw.astype(f32) + ssd_dt_bias.astype(f32)).reshape(b, T, G, J)
    A = -jnp.exp(ssd_A_log.astype(f32)).reshape(G, J)
    xh = xs.astype(f32).reshape(b, T, G, J, P)
    S0 = ssm.astype(f32).reshape(b, G, J, P, N)
    y, S = ssd_scan(xh, dt, A, Bm, Cm, S0)
    y = y + ssd_D.astype(f32).reshape(G, J)[:, :, None] * xh
    y = y.reshape(b, T, SSD_INNER).astype(u.dtype)
    y = rms_norm(y * jax.nn.silu(z), ssd_norm)

    m = jnp.concatenate([c, y], axis=-1)
    return m, new_conv_buf, new_xbc_buf, S.reshape(b, SSD_HEADS, P, N).astype(u.dtype)


def trunk_layer(x, p, conv_buf, xbc_buf, ssm, w):
    (ffn1_norm, ffn1_w_gate, ffn1_w_up, ffn1_w_down, mix_norm, w_in,
     conv_dw_w, conv_dw_b, conv_ln_g, conv_ln_b, ssd_conv_w, ssd_conv_b,
     ssd_dt_bias, ssd_A_log, ssd_D, ssd_norm, w_out,
     ffn2_norm, ffn2_w_gate, ffn2_w_up, ffn2_w_down,
     ple_norm, ple_w_proj, ple_w_gate) = w
    h = x + 0.5 * swiglu_ffn(rms_norm(x, ffn1_norm), ffn1_w_gate, ffn1_w_up, ffn1_w_down)
    u = rms_norm(h, mix_norm) @ w_in
    m, nc, nx, ns = hybrid_mixer(u, conv_buf, xbc_buf, ssm, conv_dw_w, conv_dw_b, conv_ln_g, conv_ln_b,
                                 ssd_conv_w, ssd_conv_b, ssd_dt_bias, ssd_A_log, ssd_D, ssd_norm)
    h = h + m @ w_out
    h = h + 0.5 * swiglu_ffn(rms_norm(h, ffn2_norm), ffn2_w_gate, ffn2_w_up, ffn2_w_down)
    gate = jax.nn.sigmoid(rms_norm(h, ple_norm) @ ple_w_gate)
    h = h + gate * (p.astype(h.dtype) @ ple_w_proj)
    return h, nc, nx, ns


def setup_inputs(seed: int = 0) -> dict:
    key = jax.random.key(seed)
    ks = jax.random.split(key, 40)
    f32 = jnp.float32

    def nrm(k, shape, scale):
        return jax.random.normal(k, shape, f32) * scale

    def gain(k, shape):
        return 1.0 + 0.05 * jax.random.normal(k, shape, f32)

    dt0 = jnp.exp(jax.random.uniform(ks[30], (DEPTH, SSD_HEADS), f32)
                  * (np.log(0.1) - np.log(0.001)) + np.log(0.001))
    dt_bias = dt0 + jnp.log(-jnp.expm1(-dt0))
    A_log = jnp.log(jax.random.uniform(ks[31], (DEPTH, SSD_HEADS), f32, 1.0, 16.0))

    return {
        "x_prompt": nrm(ks[0], (BATCH, SEQ, D_MODEL), 1.0),
        "x_sample": nrm(ks[1], (DEC_BATCH, DEC_SEQ, D_MODEL), 1.0),
        "p_prompt": nrm(ks[2], (DEPTH, BATCH, SEQ, D_PLE), 1.0),
        "p_sample": nrm(ks[3], (DEPTH, DEC_BATCH, DEC_SEQ, D_PLE), 1.0),
        "state_conv": nrm(ks[4], (DEPTH, DEC_BATCH, CONV_WIDTH - 1, CONV_CH), 0.5),
        "state_ssd_conv": nrm(ks[5], (DEPTH, DEC_BATCH, SSD_CONV_WIDTH - 1, SSD_CONV_DIM), 0.5),
        "state_ssd": nrm(ks[6], (DEPTH, DEC_BATCH, SSD_HEADS, SSD_HEAD_DIM, SSD_STATE), 0.5),
        "ffn1_norm": gain(ks[7], (DEPTH, D_MODEL)),
        "ffn1_w_gate": nrm(ks[8], (DEPTH, D_MODEL, D_FF), D_MODEL ** -0.5),
        "ffn1_w_up": nrm(ks[9], (DEPTH, D_MODEL, D_FF), D_MODEL ** -0.5),
        "ffn1_w_down": nrm(ks[10], (DEPTH, D_FF, D_MODEL), D_FF ** -0.5),
        "mix_norm": gain(ks[11], (DEPTH, D_MODEL)),
        "w_in": nrm(ks[12], (DEPTH, D_MODEL, IN_PROJ), D_MODEL ** -0.5),
        "conv_dw_w": nrm(ks[13], (DEPTH, CONV_WIDTH, CONV_CH), CONV_WIDTH ** -0.5),
        "conv_dw_b": nrm(ks[14], (DEPTH, CONV_CH), 0.01),
        "conv_ln_g": gain(ks[15], (DEPTH, CONV_CH)),
        "conv_ln_b": nrm(ks[16], (DEPTH, CONV_CH), 0.01),
        "ssd_conv_w": nrm(ks[17], (DEPTH, SSD_CONV_WIDTH, SSD_CONV_DIM), SSD_CONV_WIDTH ** -0.5),
        "ssd_conv_b": nrm(ks[18], (DEPTH, SSD_CONV_DIM), 0.01),
        "ssd_dt_bias": dt_bias,
        "ssd_A_log": A_log,
        "ssd_D": gain(ks[19], (DEPTH, SSD_HEADS)),
        "ssd_norm": gain(ks[20], (DEPTH, SSD_INNER)),
        "w_out": nrm(ks[21], (DEPTH, MIX_WIDTH, D_MODEL), MIX_WIDTH ** -0.5),
        "ffn2_norm": gain(ks[22], (DEPTH, D_MODEL)),
        "ffn2_w_gate": nrm(ks[23], (DEPTH, D_MODEL, D_FF), D_MODEL ** -0.5),
        "ffn2_w_up": nrm(ks[24], (DEPTH, D_MODEL, D_FF), D_MODEL ** -0.5),
        "ffn2_w_down": nrm(ks[25], (DEPTH, D_FF, D_MODEL), D_FF ** -0.5),
        "ple_norm": gain(ks[26], (DEPTH, D_MODEL)),
        "ple_w_proj": nrm(ks[27], (DEPTH, D_PLE, D_MODEL), D_PLE ** -0.5),
        "ple_w_gate": nrm(ks[28], (DEPTH, D_MODEL, D_MODEL), D_MODEL ** -0.5),
        "final_norm": gain(ks[29], (D_MODEL,)),
    }


def reference(x_prompt, x_sample, p_prompt, p_sample, state_conv, state_ssd_conv, state_ssd,
              ffn1_norm, ffn1_w_gate, ffn1_w_up, ffn1_w_down, mix_norm, w_in,
              conv_dw_w, conv_dw_b, conv_ln_g, conv_ln_b, ssd_conv_w, ssd_conv_b,
              ssd_dt_bias, ssd_A_log, ssd_D, ssd_norm, w_out,
              ffn2_norm, ffn2_w_gate, ffn2_w_up, ffn2_w_down,
              ple_norm, ple_w_proj, ple_w_gate, final_norm):
    bp = x_prompt.shape[0]
    h_p, h_s = x_prompt, x_sample
    conv_p, xbc_p, ssm_p, conv_s, xbc_s, ssm_s = [], [], [], [], [], []
    for i in range(DEPTH):
        w_i = (ffn1_norm[i], ffn1_w_gate[i], ffn1_w_up[i], ffn1_w_down[i], mix_norm[i], w_in[i],
               conv_dw_w[i], conv_dw_b[i], conv_ln_g[i], conv_ln_b[i], ssd_conv_w[i], ssd_conv_b[i],
               ssd_dt_bias[i], ssd_A_log[i], ssd_D[i], ssd_norm[i], w_out[i],
               ffn2_norm[i], ffn2_w_gate[i], ffn2_w_up[i], ffn2_w_down[i],
               ple_norm[i], ple_w_proj[i], ple_w_gate[i])
        zc = jnp.zeros((bp, CONV_WIDTH - 1, CONV_CH), x_prompt.dtype)
        zx = jnp.zeros((bp, SSD_CONV_WIDTH - 1, SSD_CONV_DIM), x_prompt.dtype)
        zs = jnp.zeros((bp, SSD_HEADS, SSD_HEAD_DIM, SSD_STATE), x_prompt.dtype)
        h_p, c1, x1, s1 = trunk_layer(h_p, p_prompt[i], zc, zx, zs, w_i)
        h_s, c2, x2, s2 = trunk_layer(h_s, p_sample[i], state_conv[i], state_ssd_conv[i], state_ssd[i], w_i)
        conv_p.append(c1); xbc_p.append(x1); ssm_p.append(s1)
        conv_s.append(c2); xbc_s.append(x2); ssm_s.append(s2)
    y_prompt = rms_norm(h_p, final_norm)
    y_sample = rms_norm(h_s, final_norm)
    return (y_prompt, y_sample,
            jnp.stack(conv_p), jnp.stack(xbc_p), jnp.stack(ssm_p),
            jnp.stack(conv_s), jnp.stack(xbc_s), jnp.stack(ssm_s))
```

```python
import functools

import jax
import jax.numpy as jnp
from jax import lax
from jax.experimental import pallas as pl
from jax.experimental.pallas import tpu as pltpu

EPS = 1e-6
CONV_WIDTH = 31
SSD_CONV_WIDTH = 4
SSD_HEADS = 16
SSD_HEAD_DIM = 64
SSD_GROUPS = 2
SSD_STATE = 128
LANES = 128
SUBLANES = 8
VMEM_LIMIT_BYTES = 56 * 1024 * 1024
CONV_HIST_ROW = 2
SSD_HIST_ROW = 5
NEG_BIG = -1e30

F32 = jnp.float32
BF16 = jnp.bfloat16


def _rms(x, g):
    ms = jnp.mean(x * x, axis=-1, keepdims=True)
    return x * lax.rsqrt(ms + EPS) * g


def _dot(a, b):
    return jnp.dot(a, b, preferred_element_type=F32)


def _split2(v):
    hi = v.astype(BF16)
    lo = (v - hi.astype(F32)).astype(BF16)
    return hi, lo


def _split3(v):
    hi = v.astype(BF16)
    r = v - hi.astype(F32)
    mid = r.astype(BF16)
    lo = (r - mid.astype(F32)).astype(BF16)
    return hi, mid, lo


def _transpose_rows(v):
    return v.T


def _swiglu(xn, wg_ref, wu_ref, wd_ref, n_chunks):
    d_ff = wg_ref.shape[1]
    ck = d_ff // n_chunks
    acc = None
    for c in range(n_chunks):
        cs = slice(c * ck, (c + 1) * ck)
        g = _dot(xn, wg_ref[:, cs])
        u = _dot(xn, wu_ref[:, cs])
        hm = (g * jax.nn.sigmoid(g) * u).astype(BF16)
        part = _dot(hm, wd_ref[cs, :])
        acc = part if acc is None else acc + part
    return acc


def _ffn1_body(x_ref, ng_ref, wg_ref, wu_ref, wd_ref, o_ref, *, n_chunks):
    x = x_ref[...]
    xn = _rms(x, ng_ref[...]).astype(BF16)
    o_ref[...] = x + 0.5 * _swiglu(xn, wg_ref, wu_ref, wd_ref, n_chunks)


def _ffn2_body(x_ref, p_ref, ng_ref, wg_ref, wu_ref, wd_ref, pn_ref, pwp_ref, pwg_ref, fn_ref, o_ref, *, n_chunks):
    x = x_ref[...]
    xn = _rms(x, ng_ref[...]).astype(BF16)
    h = x + 0.5 * _swiglu(xn, wg_ref, wu_ref, wd_ref, n_chunks)
    gate = jax.nn.sigmoid(_dot(_rms(h, pn_ref[...]).astype(BF16), pwg_ref[...]))
    h = h + gate * _dot(p_ref[...].astype(BF16), pwp_ref[...])
    o_ref[...] = _rms(h, fn_ref[...])


def _resident(shape):
    nd = len(shape)
    return pl.BlockSpec(shape, lambda *_: (0,) * nd, pipeline_mode=pl.Buffered(1))


def _token_tile(n_tokens):
    for tm in (512, 256, 128, 64, 32, 16, 8):
        if n_tokens % tm == 0:
            return tm
    raise ValueError(f"token count {n_tokens} must be a multiple of {SUBLANES}")


def _ff_chunks(d_ff):
    return 2 if d_ff % (2 * LANES) == 0 else 1


def _ffn1(x, ng, wg, wu, wd):
    n, d = x.shape
    tm = _token_tile(n)
    row = pl.BlockSpec((tm, d), lambda i: (i, 0))
    return pl.pallas_call(
        functools.partial(_ffn1_body, n_chunks=_ff_chunks(wg.shape[1])),
        grid=(n // tm,),
        in_specs=[row, _resident(ng.shape), _resident(wg.shape), _resident(wu.shape), _resident(wd.shape)],
        out_specs=row,
        out_shape=jax.ShapeDtypeStruct((n, d), F32),
        compiler_params=pltpu.CompilerParams(dimension_semantics=("arbitrary",), vmem_limit_bytes=VMEM_LIMIT_BYTES),
        name="ffn1",
    )(x, ng, wg, wu, wd)


def _ffn2(x, p, ng, wg, wu, wd, pn, pwp, pwg, fn):
    n, d = x.shape
    tm = _token_tile(n)
    row = pl.BlockSpec((tm, d), lambda i: (i, 0))
    prow = pl.BlockSpec((tm, p.shape[1]), lambda i: (i, 0))
    weights = (ng, wg, wu, wd, pn, pwp, pwg, fn)
    return pl.pallas_call(
        functools.partial(_ffn2_body, n_chunks=_ff_chunks(wg.shape[1])),
        grid=(n // tm,),
        in_specs=[row, prow] + [_resident(w.shape) for w in weights],
        out_specs=row,
        out_shape=jax.ShapeDtypeStruct((n, d), F32),
        compiler_params=pltpu.CompilerParams(dimension_semantics=("arbitrary",), vmem_limit_bytes=VMEM_LIMIT_BYTES),
        name="ffn2",
    )(x, p, *weights)


def _mixer_body(h_ref, cbuf_ref, xbuf_ref, ssm_ref, mixg_ref, win_ref, cw_ref, cb_ref, lng_ref, lnb_ref,
                sw_ref, sb_ref, dtb_ref, alog_ref, dexp_ref, sng_ref, hexp_ref, wout_ref,
                o_ref, ncb_ref, nxb_ref, nss_ref, apad, xpad, st, *, tl):
    c = pl.program_id(1)
    conv_ch = cw_ref.shape[1]
    inner = SSD_HEADS * SSD_HEAD_DIM
    gn = SSD_GROUPS * SSD_STATE
    gw = inner // SSD_GROUPS
    lp = max(tl, LANES)
    a0 = CONV_HIST_ROW + CONV_WIDTH - 1
    x0 = SSD_HIST_ROW + SSD_CONV_WIDTH - 1

    @pl.when(c == 0)
    def _():
        apad[0:CONV_HIST_ROW, :] = jnp.zeros((CONV_HIST_ROW, conv_ch), F32)
        apad[CONV_HIST_ROW:a0, :] = cbuf_ref[...]
        xpad[0:SSD_HIST_ROW, :] = jnp.zeros((SSD_HIST_ROW, xpad.shape[1]), F32)
        xpad[SSD_HIST_ROW:x0, :] = xbuf_ref[...]
        st[...] = ssm_ref[...].T

    h = h_ref[...]
    hn = _rms(h, mixg_ref[...]).astype(BF16)
    o1, o2, o3 = conv_ch, 2 * conv_ch, 2 * conv_ch + inner
    o4 = o3 + inner + 2 * gn

    cu = _dot(hn, win_ref[:, 0:o2])
    apad[a0:a0 + tl, :] = cu[:, 0:o1] * jax.nn.sigmoid(cu[:, o1:o2])
    cparts = []
    for cbk in range(conv_ch // LANES):
        cs = slice(cbk * LANES, (cbk + 1) * LANES)
        acc = jnp.broadcast_to(cb_ref[:, cs], (tl, LANES))
        for k in range(CONV_WIDTH):
            acc = acc + cw_ref[k:k + 1, cs] * apad[CONV_HIST_ROW + k:CONV_HIST_ROW + k + tl, cs]
        cparts.append(acc)
    conv = jnp.concatenate(cparts, axis=1)
    mu = jnp.mean(conv, axis=-1, keepdims=True)
    xc = conv - mu
    cn = xc * lax.rsqrt(jnp.mean(xc * xc, axis=-1, keepdims=True) + EPS) * lng_ref[...] + lnb_ref[...]
    cact = cn * jax.nn.sigmoid(cn)

    xpad[x0:x0 + tl, :] = _dot(hn, win_ref[:, o3:o4])
    xb = jnp.broadcast_to(sb_ref[...], (tl, xpad.shape[1]))
    for k in range(SSD_CONV_WIDTH):
        xb = xb + sw_ref[k:k + 1, :] * xpad[SSD_HIST_ROW + k:SSD_HIST_ROW + k + tl, :]
    xb = xb * jax.nn.sigmoid(xb)

    lane = lax.broadcasted_iota(jnp.int32, (1, LANES), 1)
    head_lane = lane < SSD_HEADS
    dtr = _dot(hn, win_ref[:, o4:o4 + LANES]) + dtb_ref[...]
    dt = jnp.maximum(dtr, 0.0) + jnp.log1p(jnp.exp(-jnp.abs(dtr)))
    dt = jnp.where(head_lane, dt, 0.0)
    a_neg = jnp.where(head_lane, -jnp.exp(alog_ref[...]), 0.0)

    def pad_rows(v):
        if lp == tl:
            return v
        return jnp.concatenate([v, jnp.zeros((lp - tl, v.shape[1]), v.dtype)], axis=0)

    xs = pad_rows(xb[:, 0:inner])
    bm = pad_rows(xb[:, inner:inner + gn])
    cm = pad_rows(xb[:, inner + gn:inner + 2 * gn])
    dt = pad_rows(dt)

    row_i = lax.broadcasted_iota(jnp.int32, (lp, lp), 0)
    col_i = lax.broadcasted_iota(jnp.int32, (lp, lp), 1)
    causal = row_i >= col_i
    tril = jnp.where(causal, 1.0, 0.0).astype(BF16)

    d_hi, d_mid, d_lo = _split3(dt * a_neg)
    acum = _dot(tril, d_hi) + _dot(tril, d_mid) + _dot(tril, d_lo)
    alast = acum[lp - 1:lp, :]
    acum_t = _transpose_rows(acum)
    dt_t = _transpose_rows(dt)
    ea = jnp.exp(acum)
    wld = jnp.exp(alast - acum) * dt
    eal = jnp.broadcast_to(jnp.exp(alast), (SUBLANES, LANES))

    def expand(v):
        hi, lo = _split2(v)
        return _dot(hi, hexp_ref[...]) + _dot(lo, hexp_ref[...])

    ea_x = expand(ea)
    wld_x = expand(wld)
    eal_x = expand(eal)[0:1, :]

    xs16 = xs.astype(BF16)
    xw16 = (wld_x * xs).astype(BF16)
    half = lax.broadcasted_iota(jnp.int32, (1, LANES), 1) < SSD_HEAD_DIM
    pairs_per_group = gw // LANES
    yparts = []
    for g in range(SSD_GROUPS):
        gs = slice(g * gw, (g + 1) * gw)
        ns = slice(g * SSD_STATE, (g + 1) * SSD_STATE)
        b_t16 = _transpose_rows(bm[:, ns]).astype(BF16)
        c16 = cm[:, ns].astype(BF16)
        cb = _dot(c16, b_t16)
        st_g = st[:, gs]
        y_inter = _dot(c16, st_g.astype(BF16))
        st[:, gs] = st_g * eal_x[:, gs] + _dot(b_t16, xw16[:, gs])
        for jp in range(pairs_per_group):
            pair = g * pairs_per_group + jp
            ls = slice(pair * LANES, (pair + 1) * LANES)
            ys = []
            for j in (2 * pair, 2 * pair + 1):
                diff = acum[:, j:j + 1] - acum_t[j:j + 1, :]
                mj = cb * jnp.exp(jnp.where(causal, diff, NEG_BIG)) * dt_t[j:j + 1, :]
                ys.append(_dot(mj.astype(BF16), xs16[:, ls]))
            y_intra = jnp.where(half, ys[0], ys[1])
            yparts.append(y_intra + y_inter[:, jp * LANES:(jp + 1) * LANES] * ea_x[:, ls])
    y = jnp.concatenate(yparts, axis=1)[0:tl, :] + dexp_ref[...] * xs[0:tl, :]
    z = _dot(hn, win_ref[:, o2:o3])
    y = _rms(y * (z * jax.nn.sigmoid(z)), sng_ref[...])

    o_ref[...] = h + _dot(cact.astype(BF16), wout_ref[0:conv_ch, :]) + _dot(y.astype(BF16), wout_ref[conv_ch:, :])

    apad[CONV_HIST_ROW:a0, :] = apad[CONV_HIST_ROW + tl:a0 + tl, :]
    xpad[SSD_HIST_ROW:x0, :] = xpad[SSD_HIST_ROW + tl:x0 + tl, :]

    @pl.when(c == pl.num_programs(1) - 1)
    def _():
        ncb_ref[...] = apad[CONV_HIST_ROW:a0, :]
        nxb_ref[...] = xpad[SSD_HIST_ROW:x0, :]
        nss_ref[...] = st[...].T


def _mixer(h, conv_buf, xbc_buf, ssm, weights):
    b, t, d = h.shape
    tl = 256 if t % 256 == 0 else t
    assert tl % SUBLANES == 0 and tl >= CONV_WIDTH - 1
    conv_ch = conv_buf.shape[-1]
    xbc_dim = xbc_buf.shape[-1]
    inner = SSD_HEADS * SSD_HEAD_DIM
    ssm2 = ssm.reshape(b, inner, SSD_STATE)
    seq = lambda rows, cols: pl.BlockSpec((None, rows, cols), lambda i, c: (i, 0, 0))
    out = pl.pallas_call(
        functools.partial(_mixer_body, tl=tl),
        grid=(b, t // tl),
        in_specs=[pl.BlockSpec((None, tl, d), lambda i, c: (i, c, 0)),
                  seq(CONV_WIDTH - 1, conv_ch), seq(SSD_CONV_WIDTH - 1, xbc_dim), seq(inner, SSD_STATE)]
                 + [_resident(w.shape) for w in weights],
        out_specs=[pl.BlockSpec((None, tl, d), lambda i, c: (i, c, 0)),
                   seq(CONV_WIDTH - 1, conv_ch), seq(SSD_CONV_WIDTH - 1, xbc_dim), seq(inner, SSD_STATE)],
        out_shape=[jax.ShapeDtypeStruct((b, t, d), F32),
                   jax.ShapeDtypeStruct(conv_buf.shape, F32),
                   jax.ShapeDtypeStruct(xbc_buf.shape, F32),
                   jax.ShapeDtypeStruct(ssm2.shape, F32)],
        scratch_shapes=[pltpu.VMEM((CONV_HIST_ROW + CONV_WIDTH - 1 + tl, conv_ch), F32),
                        pltpu.VMEM((SSD_HIST_ROW + SSD_CONV_WIDTH - 1 + tl, xbc_dim), F32),
                        pltpu.VMEM((SSD_STATE, inner), F32)],
        compiler_params=pltpu.CompilerParams(dimension_semantics=("arbitrary", "arbitrary"),
                                             vmem_limit_bytes=VMEM_LIMIT_BYTES),
        name="mixer",
    )(h, conv_buf, xbc_buf, ssm2, *weights)
    h2, ncb, nxb, nss = out
    return h2, ncb, nxb, nss.reshape(ssm.shape)


def _row(v):
    return v.reshape(1, -1).astype(F32)


def _pad_lanes(v, width):
    return jnp.pad(v, ((0, 0), (0, width - v.shape[1])))


def kernel(x_prompt, x_sample, p_prompt, p_sample, state_conv, state_ssd_conv, state_ssd,
           ffn1_norm, ffn1_w_gate, ffn1_w_up, ffn1_w_down, mix_norm, w_in,
           conv_dw_w, conv_dw_b, conv_ln_g, conv_ln_b, ssd_conv_w, ssd_conv_b,
           ssd_dt_bias, ssd_A_log, ssd_D, ssd_norm, w_out,
           ffn2_norm, ffn2_w_gate, ffn2_w_up, ffn2_w_down,
           ple_norm, ple_w_proj, ple_w_gate, final_norm):
    depth = ffn1_norm.shape[0]
    assert depth == 1, "the FFN2 kernel fuses the final norm, so it must be the last layer's"
    bp, sp, d = x_prompt.shape
    bs, ss, _ = x_sample.shape
    conv_ch = state_conv.shape[-1]
    xbc_dim = state_ssd_conv.shape[-1]
    head_expand = (jnp.arange(LANES)[:, None] == jnp.arange(SSD_HEADS * SSD_HEAD_DIM)[None, :] // SSD_HEAD_DIM).astype(BF16)

    h_p, h_s = x_prompt, x_sample
    states = [[] for _ in range(6)]
    for i in range(depth):
        in_w = w_in[i].shape[1]
        win = _pad_lanes(w_in[i], in_w - SSD_HEADS + LANES).astype(BF16)
        mix_w = (_row(mix_norm[i]), win, conv_dw_w[i], _row(conv_dw_b[i]), _row(conv_ln_g[i]), _row(conv_ln_b[i]),
                 ssd_conv_w[i], _row(ssd_conv_b[i]), _pad_lanes(_row(ssd_dt_bias[i]), LANES),
                 _pad_lanes(_row(ssd_A_log[i]), LANES), _row(jnp.repeat(ssd_D[i], SSD_HEAD_DIM)),
                 _row(ssd_norm[i]), head_expand, w_out[i].astype(BF16))
        f1_w = (_row(ffn1_norm[i]), ffn1_w_gate[i].astype(BF16), ffn1_w_up[i].astype(BF16), ffn1_w_down[i].astype(BF16))
        f2_w = (_row(ffn2_norm[i]), ffn2_w_gate[i].astype(BF16), ffn2_w_up[i].astype(BF16), ffn2_w_down[i].astype(BF16),
                _row(ple_norm[i]), ple_w_proj[i].astype(BF16), ple_w_gate[i].astype(BF16))
        zero_hist = (jnp.zeros((bp, CONV_WIDTH - 1, conv_ch), F32), jnp.zeros((bp, SSD_CONV_WIDTH - 1, xbc_dim), F32),
                     jnp.zeros((bp,) + state_ssd.shape[2:], F32))
        outs = []
        for hcur, p, hist in ((h_p, p_prompt[i], zero_hist),
                              (h_s, p_sample[i], (state_conv[i], state_ssd_conv[i], state_ssd[i]))):
            b, t, _ = hcur.shape
            h1 = _ffn1(hcur.reshape(b * t, d), *f1_w).reshape(b, t, d)
            h2, ncb, nxb, nss = _mixer(h1, *hist, mix_w)
            outs.append((h2, p, ncb, nxb, nss))
        (h_p, pp, c1, x1, s1), (h_s, ps, c2, x2, s2) = outs
        for lst, v in zip(states, (c1, x1, s1, c2, x2, s2)):
            lst.append(v)
        h_p =_ffn2(h_p.reshape(bp * sp, d), pp.reshape(bp * sp, -1), *f2_w, _row(final_norm)).reshape(bp, sp, d)
        h_s = _ffn2(h_s.reshape(bs * ss, d), ps.reshape(bs * ss, -1), *f2_w, _row(final_norm)).reshape(bs, ss, d)
    return (h_p, h_s) + tuple(jnp.stack(s) for s in states)
```

```python
import functools

import jax
import jax.numpy as jnp
from jax import lax
from jax.experimental import pallas as pl
from jax.experimental.pallas import tpu as pltpu

EPS = 1e-6
CONV_WIDTH = 31
SSD_CONV_WIDTH = 4
SSD_HEADS = 16
SSD_HEAD_DIM = 64
SSD_GROUPS = 2
SSD_STATE = 128
LANES = 128
SUBLANES = 8
MXU_DIM = 256
VMEM_LIMIT_BYTES = 56 * 1024 * 1024
CONV_HIST_ROWS = 32
SSD_HIST_ROWS = 8
NEG_BIG = -1e30

F32 = jnp.float32
BF16 = jnp.bfloat16


def _rms(x, g):
    ms = jnp.mean(x * x, axis=-1, keepdims=True)
    return x * lax.rsqrt(ms + EPS) * g


def _dot(a, b):
    return jnp.dot(a, b, preferred_element_type=F32)


def _split2(v):
    hi = v.astype(BF16)
    lo = (v - hi.astype(F32)).astype(BF16)
    return hi, lo


def _split3(v):
    hi = v.astype(BF16)
    r = v - hi.astype(F32)
    mid = r.astype(BF16)
    lo = (r - mid.astype(F32)).astype(BF16)
    return hi, mid, lo


def _causal_taps(ext, hist_rows, w_ref, cs, n_taps, tl, acc):
    rolled = {}
    for k in range(n_taps):
        off = hist_rows - (n_taps - 1) + k
        r = (-off) % SUBLANES
        if r not in rolled:
            rolled[r] = ext if r == 0 else pltpu.roll(ext, r, axis=0)
        start = off + r
        acc = acc + w_ref[k:k + 1, cs] * rolled[r][start:start + tl, :]
    return acc


def _ff_chunks(d_ff, n_chunks=4):
    tiles = d_ff // MXU_DIM
    assert tiles * MXU_DIM == d_ff and tiles >= n_chunks
    edges = [MXU_DIM * ((tiles * c + n_chunks - 1) // n_chunks) for c in range(n_chunks + 1)]
    return tuple(slice(lo, hi) for lo, hi in zip(edges[:-1], edges[1:]))


def _swiglu(xn, wg_ref, wu_ref, wd_ref):
    acc = None
    for cs in _ff_chunks(wg_ref.shape[1]):
        g = _dot(xn, wg_ref[:, cs])
        u = _dot(xn, wu_ref[:, cs])
        hm = (g * jax.nn.sigmoid(g) * u).astype(BF16)
        part = _dot(hm, wd_ref[cs, :])
        acc = part if acc is None else acc + part
    return acc


def _ffn1_body(x_ref, ng_ref, wg_ref, wu_ref, wd_ref, o_ref):
    x = x_ref[...]
    xn = _rms(x, ng_ref[...]).astype(BF16)
    o_ref[...] = x + 0.5 * _swiglu(xn, wg_ref, wu_ref, wd_ref)


def _ffn2_body(x_ref, p_ref, ng_ref, wg_ref, wu_ref, wd_ref, pn_ref, pwp_ref, pwg_ref, fn_ref, o_ref):
    x = x_ref[...]
    xn = _rms(x, ng_ref[...]).astype(BF16)
    h = x + 0.5 * _swiglu(xn, wg_ref, wu_ref, wd_ref)
    gate = jax.nn.sigmoid(_dot(_rms(h, pn_ref[...]).astype(BF16), pwg_ref[...]))
    h = h + gate * _dot(p_ref[...].astype(BF16), pwp_ref[...])
    o_ref[...] = _rms(h, fn_ref[...])


def _resident(shape):
    nd = len(shape)
    return pl.BlockSpec(shape, lambda *_: (0,) * nd, pipeline_mode=pl.Buffered(1))


def _token_tile(n_tokens):
    for tm in (512, 256, 128, 64, 32, 16, 8):
        if n_tokens % tm == 0:
            return tm
    raise ValueError(f"token count {n_tokens} must be a multiple of {SUBLANES}")


def _ffn1(x, ng, wg, wu, wd):
    n, d = x.shape
    tm = _token_tile(n)
    row = pl.BlockSpec((tm, d), lambda i: (i, 0))
    return pl.pallas_call(
        _ffn1_body,
        grid=(n // tm,),
        in_specs=[row, _resident(ng.shape), _resident(wg.shape), _resident(wu.shape), _resident(wd.shape)],
        out_specs=row,
        out_shape=jax.ShapeDtypeStruct((n, d), F32),
        compiler_params=pltpu.CompilerParams(dimension_semantics=("arbitrary",), vmem_limit_bytes=VMEM_LIMIT_BYTES),
        name="ffn1",
    )(x, ng, wg, wu, wd)


def _ffn2(x, p, ng, wg, wu, wd, pn, pwp, pwg, fn):
    n, d = x.shape
    tm = _token_tile(n)
    row = pl.BlockSpec((tm, d), lambda i: (i, 0))
    prow = pl.BlockSpec((tm, p.shape[1]), lambda i: (i, 0))
    weights = (ng, wg, wu, wd, pn, pwp, pwg, fn)
    return pl.pallas_call(
        _ffn2_body,
        grid=(n // tm,),
        in_specs=[row, prow] + [_resident(w.shape) for w in weights],
        out_specs=row,
        out_shape=jax.ShapeDtypeStruct((n, d), F32),
        compiler_params=pltpu.CompilerParams(dimension_semantics=("arbitrary",), vmem_limit_bytes=VMEM_LIMIT_BYTES),
        name="ffn2",
    )(x, p, *weights)


def _mixer_body(h_ref, cbuf_ref, xbuf_ref, ssm_ref, mixg_ref, win_ref, cw_ref, cb_ref, lng_ref, lnb_ref,
                sw_ref, sb_ref, dtb_ref, alog_ref, dexp_ref, sng_ref, hexp_ref, wout_ref,
                o_ref, ncb_ref, nxb_ref, nss_ref, apad, xpad, st, *, tl):
    c = pl.program_id(1)
    conv_ch = cw_ref.shape[1]
    xbc_dim = sw_ref.shape[1]
    inner = SSD_HEADS * SSD_HEAD_DIM
    gn = SSD_GROUPS * SSD_STATE
    gw = inner // SSD_GROUPS
    lp = max(tl, LANES)
    ch0 = CONV_HIST_ROWS - (CONV_WIDTH - 1)
    xh0 = SSD_HIST_ROWS - (SSD_CONV_WIDTH - 1)

    @pl.when(c == 0)
    def _():
        apad[0:ch0, :] = jnp.zeros((ch0, conv_ch), F32)
        apad[ch0:CONV_HIST_ROWS, :] = cbuf_ref[...]
        xpad[0:xh0, :] = jnp.zeros((xh0, xbc_dim), F32)
        xpad[xh0:SSD_HIST_ROWS, :] = xbuf_ref[...]
        st[...] = ssm_ref[...].T

    h = h_ref[...]
    hn = _rms(h, mixg_ref[...]).astype(BF16)
    o1, o2, o3 = conv_ch, 2 * conv_ch, 2 * conv_ch + inner
    o4 = o3 + xbc_dim

    cu = _dot(hn, win_ref[:, 0:o2])
    apad[CONV_HIST_ROWS:, :] = cu[:, 0:o1] * jax.nn.sigmoid(cu[:, o1:o2])
    cparts = []
    for cbk in range(conv_ch // LANES):
        cs = slice(cbk * LANES, (cbk + 1) * LANES)
        acc = jnp.broadcast_to(cb_ref[:, cs], (tl, LANES))
        cparts.append(_causal_taps(apad[:, cs], CONV_HIST_ROWS, cw_ref, cs, CONV_WIDTH, tl, acc))
    conv = jnp.concatenate(cparts, axis=1)
    mu = jnp.mean(conv, axis=-1, keepdims=True)
    xc = conv - mu
    cn = xc * lax.rsqrt(jnp.mean(xc * xc, axis=-1, keepdims=True) + EPS) * lng_ref[...] + lnb_ref[...]
    cact = cn * jax.nn.sigmoid(cn)

    xpad[SSD_HIST_ROWS:, :] = _dot(hn, win_ref[:, o3:o4])
    xparts = []
    for cbk in range(xbc_dim // LANES):
        cs = slice(cbk * LANES, (cbk + 1) * LANES)
        acc = jnp.broadcast_to(sb_ref[:, cs], (tl, LANES))
        xparts.append(_causal_taps(xpad[:, cs], SSD_HIST_ROWS, sw_ref, cs, SSD_CONV_WIDTH, tl, acc))
    xb = jnp.concatenate(xparts, axis=1)
    xb = xb * jax.nn.sigmoid(xb)

    lane = lax.broadcasted_iota(jnp.int32, (1, LANES), 1)
    head_lane = lane < SSD_HEADS
    dtr = _dot(hn, win_ref[:, o4:o4 + LANES]) + dtb_ref[...]
    dt = jnp.maximum(dtr, 0.0) + jnp.log1p(jnp.exp(-jnp.abs(dtr)))
    dt = jnp.where(head_lane, dt, 0.0)
    a_neg = jnp.where(head_lane, -jnp.exp(alog_ref[...]), 0.0)

    def pad_rows(v):
        if lp == tl:
            return v
        return jnp.concatenate([v, jnp.zeros((lp - tl, v.shape[1]), v.dtype)], axis=0)

    xs = pad_rows(xb[:, 0:inner])
    bm = pad_rows(xb[:, inner:inner + gn])
    cm = pad_rows(xb[:, inner + gn:inner + 2 * gn])
    dt = pad_rows(dt)

    row_i = lax.broadcasted_iota(jnp.int32, (lp, lp), 0)
    col_i = lax.broadcasted_iota(jnp.int32, (lp, lp), 1)
    causal = row_i >= col_i
    tril = jnp.where(causal, 1.0, 0.0).astype(BF16)

    d_hi, d_mid, d_lo = _split3(dt * a_neg)
    acum = _dot(tril, d_hi) + _dot(tril, d_mid) + _dot(tril, d_lo)
    alast = acum[lp - 1:lp, :]
    acum_t = acum.T
    dt_t = dt.T
    ea = jnp.exp(acum)
    wld = jnp.exp(alast - acum) * dt
    eal = jnp.broadcast_to(jnp.exp(alast), (SUBLANES, LANES))

    def expand(v):
        hi, lo = _split2(v)
        return _dot(hi, hexp_ref[...]) + _dot(lo, hexp_ref[...])

    ea_x = expand(ea)
    wld_x = expand(wld)
    eal_x = expand(eal)[0:1, :]

    xs16 = xs.astype(BF16)
    xw16 = (wld_x * xs).astype(BF16)
    half = lane < SSD_HEAD_DIM
    pairs_per_group = gw // LANES
    yparts = []
    for g in range(SSD_GROUPS):
        gs = slice(g * gw, (g + 1) * gw)
        ns = slice(g * SSD_STATE, (g + 1) * SSD_STATE)
        b_t16 = bm[:, ns].T.astype(BF16)
        c16 = cm[:, ns].astype(BF16)
        cb = _dot(c16, b_t16)
        st_g = st[:, gs]
        y_inter = _dot(c16, st_g.astype(BF16))
        st[:, gs] = st_g * eal_x[:, gs] + _dot(b_t16, xw16[:, gs])
        for jp in range(pairs_per_group):
            pair = g * pairs_per_group + jp
            ls = slice(pair * LANES, (pair + 1) * LANES)
            ys = []
            for j in (2 * pair, 2 * pair + 1):
                diff = acum[:, j:j + 1] - acum_t[j:j + 1, :]
                mj = cb * jnp.exp(jnp.where(causal, diff, NEG_BIG)) * dt_t[j:j + 1, :]
                ys.append(_dot(mj.astype(BF16), xs16[:, ls]))
            y_intra = jnp.where(half, ys[0], ys[1])
            yparts.append(y_intra + y_inter[:, jp * LANES:(jp + 1) * LANES] * ea_x[:, ls])
    y = jnp.concatenate(yparts, axis=1)[0:tl, :] + dexp_ref[...] * xs[0:tl, :]
    z = _dot(hn, win_ref[:, o2:o3])
    y = _rms(y * (z * jax.nn.sigmoid(z)), sng_ref[...])

    o_ref[...] = h + _dot(cact.astype(BF16), wout_ref[0:conv_ch, :]) + _dot(y.astype(BF16), wout_ref[conv_ch:, :])

    apad[0:CONV_HIST_ROWS, :] = apad[tl:tl + CONV_HIST_ROWS, :]
    xpad[0:SSD_HIST_ROWS, :] = xpad[tl:tl + SSD_HIST_ROWS, :]

    @pl.when(c == pl.num_programs(1) - 1)
    def _():
        ncb_ref[...] = apad[ch0:CONV_HIST_ROWS, :]
        nxb_ref[...] = xpad[xh0:SSD_HIST_ROWS, :]
        nss_ref[...] = st[...].T


def _mixer(h, conv_buf, xbc_buf, ssm, weights):
    b, t, d = h.shape
    tl = 256 if t % 256 == 0 else t
    assert tl % SUBLANES == 0 and tl >= CONV_HIST_ROWS
    conv_ch = conv_buf.shape[-1]
    xbc_dim = xbc_buf.shape[-1]
    inner = SSD_HEADS * SSD_HEAD_DIM
    ssm2 = ssm.reshape(b, inner, SSD_STATE)
    seq = lambda rows, cols: pl.BlockSpec((None, rows, cols), lambda i, c: (i, 0, 0))
    out = pl.pallas_call(
        functools.partial(_mixer_body, tl=tl),
        grid=(b, t // tl),
        in_specs=[pl.BlockSpec((None, tl, d), lambda i, c: (i, c, 0)),
                  seq(CONV_WIDTH - 1, conv_ch), seq(SSD_CONV_WIDTH - 1, xbc_dim), seq(inner, SSD_STATE)]
                 + [_resident(w.shape) for w in weights],
        out_specs=[pl.BlockSpec((None, tl, d), lambda i, c: (i, c, 0)),
                   seq(CONV_WIDTH - 1, conv_ch), seq(SSD_CONV_WIDTH - 1, xbc_dim), seq(inner, SSD_STATE)],
        out_shape=[jax.ShapeDtypeStruct((b, t, d), F32),
                   jax.ShapeDtypeStruct(conv_buf.shape, F32),
                   jax.ShapeDtypeStruct(xbc_buf.shape, F32),
                   jax.ShapeDtypeStruct(ssm2.shape, F32)],
        scratch_shapes=[pltpu.VMEM((CONV_HIST_ROWS + tl, conv_ch), F32),
                        pltpu.VMEM((SSD_HIST_ROWS + tl, xbc_dim), F32),
                        pltpu.VMEM((SSD_STATE, inner), F32)],
        compiler_params=pltpu.CompilerParams(dimension_semantics=("arbitrary", "arbitrary"),
                                             vmem_limit_bytes=VMEM_LIMIT_BYTES),
        name="mixer",
    )(h, conv_buf, xbc_buf, ssm2, *weights)
    h2, ncb, nxb, nss = out
    return h2, ncb, nxb, nss.reshape(ssm.shape)


def _row(v):
    return v.reshape(1, -1).astype(F32)


def _pad_lanes(v, width):
    return jnp.pad(v, ((0, 0), (0, width - v.shape[1])))


def kernel(x_prompt, x_sample, p_prompt, p_sample, state_conv, state_ssd_conv, state_ssd,
           ffn1_norm, ffn1_w_gate, ffn1_w_up, ffn1_w_down, mix_norm, w_in,
           conv_dw_w, conv_dw_b, conv_ln_g, conv_ln_b, ssd_conv_w, ssd_conv_b,
           ssd_dt_bias, ssd_A_log, ssd_D, ssd_norm, w_out,
           ffn2_norm, ffn2_w_gate, ffn2_w_up, ffn2_w_down,
           ple_norm, ple_w_proj, ple_w_gate, final_norm):
    depth = ffn1_norm.shape[0]
    assert depth == 1, "the FFN2 kernel fuses the final norm, so it must be the last layer's"
    i = 0
    bp, sp, d = x_prompt.shape
    conv_ch = state_conv.shape[-1]
    xbc_dim = state_ssd_conv.shape[-1]
    head_expand = (jnp.arange(LANES)[:, None] == jnp.arange(SSD_HEADS * SSD_HEAD_DIM)[None, :] // SSD_HEAD_DIM).astype(BF16)

    win = _pad_lanes(w_in[i], w_in.shape[2] - SSD_HEADS + LANES).astype(BF16)
    mix_w = (_row(mix_norm[i]), win, conv_dw_w[i], _row(conv_dw_b[i]), _row(conv_ln_g[i]), _row(conv_ln_b[i]),
             ssd_conv_w[i], _row(ssd_conv_b[i]), _pad_lanes(_row(ssd_dt_bias[i]), LANES),
             _pad_lanes(_row(ssd_A_log[i]), LANES), _row(jnp.repeat(ssd_D[i], SSD_HEAD_DIM)),
             _row(ssd_norm[i]), head_expand, w_out[i].astype(BF16))
    f1_w = (_row(ffn1_norm[i]), ffn1_w_gate[i].astype(BF16), ffn1_w_up[i].astype(BF16), ffn1_w_down[i].astype(BF16))
    f2_w = (_row(ffn2_norm[i]), ffn2_w_gate[i].astype(BF16), ffn2_w_up[i].astype(BF16), ffn2_w_down[i].astype(BF16),
            _row(ple_norm[i]), ple_w_proj[i].astype(BF16), ple_w_gate[i].astype(BF16), _row(final_norm))
    zero_hist = (jnp.zeros((bp, CONV_WIDTH - 1, conv_ch), F32), jnp.zeros((bp, SSD_CONV_WIDTH - 1, xbc_dim), F32),
                 jnp.zeros((bp,) + state_ssd.shape[2:], F32))
    outs = []
    for x, p, hist in ((x_prompt, p_prompt[i], zero_hist),
                       (x_sample, p_sample[i], (state_conv[i], state_ssd_conv[i], state_ssd[i]))):
        b, t, _ = x.shape
        h1 = _ffn1(x.reshape(b * t, d), *f1_w).reshape(b, t, d)
        h2, ncb, nxb, nss = _mixer(h1, *hist, mix_w)
        y = _ffn2(h2.reshape(b * t, d), p.reshape(b * t, -1), *f2_w).reshape(b, t, d)
        outs.append((y, ncb[None], nxb[None], nss[None]))
    (y_p, c1, x1, s1), (y_s, c2, x2, s2) = outs
    return (y_p, y_s, c1, x1, s1, c2, x2, s2)
```

```python
import functools

import jax
import jax.numpy as jnp
from jax import lax
from jax.experimental import pallas as pl
from jax.experimental.pallas import tpu as pltpu

EPS = 1e-6
CONV_WIDTH = 31
SSD_CONV_WIDTH = 4
SSD_HEADS = 16
SSD_HEAD_DIM = 64
SSD_GROUPS = 2
SSD_STATE = 128
LANES = 128
SUBLANES = 8
MXU_DIM = 256
VMEM_LIMIT_BYTES = 56 * 1024 * 1024
CONV_HIST_ROWS = 32
SSD_HIST_ROWS = 8
NEG_BIG = -1e30
TAP_ROWS = 128

F32 = jnp.float32
BF16 = jnp.bfloat16


def _rms(x, g):
    ms = jnp.mean(x * x, axis=-1, keepdims=True)
    return x * lax.rsqrt(ms + EPS) * g


def _dot(a, b):
    return jnp.dot(a, b, preferred_element_type=F32)


def _split2(v):
    hi = v.astype(BF16)
    lo = (v - hi.astype(F32)).astype(BF16)
    return hi, lo


def _split3(v):
    hi = v.astype(BF16)
    r = v - hi.astype(F32)
    mid = r.astype(BF16)
    lo = (r - mid.astype(F32)).astype(BF16)
    return hi, mid, lo


def _causal_taps(ext_ref, cs, hist_rows, w_ref, b_ref, n_taps, tl, roll_ref, out_ref, act=None):
    rows = hist_rows + tl
    ext = ext_ref[:, cs]
    taps = []
    for k in range(n_taps):
        off = hist_rows - (n_taps - 1) + k
        r = (-off) % SUBLANES
        taps.append((k, r, off + r))
    for r in sorted({r for _, r, _ in taps} - {0}):
        roll_ref[r - 1, 0:rows, :] = pltpu.roll(ext, r, axis=0)
    for t0 in range(0, tl, TAP_ROWS):
        nr = min(TAP_ROWS, tl - t0)
        acc = jnp.broadcast_to(b_ref[:, cs], (nr, LANES))
        for k, r, start in taps:
            lo = start + t0
            win = ext_ref[lo:lo + nr, cs] if r == 0 else roll_ref[r - 1, lo:lo + nr, :]
            acc = acc + w_ref[k:k + 1, cs] * win
        out_ref[t0:t0 + nr, cs] = acc if act is None else act(acc)


def _ff_chunks(d_ff, n_chunks=4):
    tiles = d_ff // MXU_DIM
    assert tiles * MXU_DIM == d_ff and tiles >= n_chunks
    edges = [MXU_DIM * ((tiles * c + n_chunks - 1) // n_chunks) for c in range(n_chunks + 1)]
    return tuple(slice(lo, hi) for lo, hi in zip(edges[:-1], edges[1:]))


def _swiglu(xn, wg_ref, wu_ref, wd_ref):
    acc = None
    for cs in _ff_chunks(wg_ref.shape[1]):
        g = _dot(xn, wg_ref[:, cs])
        u = _dot(xn, wu_ref[:, cs])
        hm = (g * jax.nn.sigmoid(g) * u).astype(BF16)
        part = _dot(hm, wd_ref[cs, :])
        acc = part if acc is None else acc + part
    return acc


def _ffn2_body(x_ref, p_ref, ng_ref, wg_ref, wu_ref, wd_ref, pn_ref, pwp_ref, pwg_ref, fn_ref, o_ref):
    x = x_ref[...]
    xn = _rms(x, ng_ref[...]).astype(BF16)
    h = x + 0.5 * _swiglu(xn, wg_ref, wu_ref, wd_ref)
    gate = jax.nn.sigmoid(_dot(_rms(h, pn_ref[...]).astype(BF16), pwg_ref[...]))
    h = h + gate * _dot(p_ref[...].astype(BF16), pwp_ref[...])
    o_ref[...] = _rms(h, fn_ref[...])


def _resident(shape):
    nd = len(shape)
    return pl.BlockSpec(shape, lambda *_: (0,) * nd, pipeline_mode=pl.Buffered(1))


def _token_tile(n_tokens):
    for tm in (512, 256, 128, 64, 32, 16, 8):
        if n_tokens % tm == 0:
            return tm
    raise ValueError(f"token count {n_tokens} must be a multiple of {SUBLANES}")


def _ffn2(x, p, ng, wg, wu, wd, pn, pwp, pwg, fn):
    n, d = x.shape
    tm = _token_tile(n)
    row = pl.BlockSpec((tm, d), lambda i: (i, 0))
    prow = pl.BlockSpec((tm, p.shape[1]), lambda i: (i, 0))
    weights = (ng, wg, wu, wd, pn, pwp, pwg, fn)
    return pl.pallas_call(
        _ffn2_body,
        grid=(n // tm,),
        in_specs=[row, prow] + [_resident(w.shape) for w in weights],
        out_specs=row,
        out_shape=jax.ShapeDtypeStruct((n, d), F32),
        compiler_params=pltpu.CompilerParams(dimension_semantics=("arbitrary",), vmem_limit_bytes=VMEM_LIMIT_BYTES),
        name="ffn2",
    )(x, p, *weights)


def _ffn1_mixer_body(x_ref, cbuf_ref, xbuf_ref, ssm_ref, f1g_ref, f1wg_ref, f1wu_ref, f1wd_ref,
                     mixg_ref, win_ref, cw_ref, cb_ref, lng_ref, lnb_ref,
                     sw_ref, sb_ref, dtb_ref, alog_ref, dexp_ref, sng_ref, hexp_ref, wout_ref,
                     o_ref, ncb_ref, nxb_ref, nss_ref,
                     h1s, apad, xpad, st, hn_s, xn_s, rolls, cout, xbs, facc, ybuf, *, tl, chunks_per_seq):
    s = pl.program_id(0)
    c = jnp.maximum(s - 1, 0) % chunks_per_seq
    conv_ch = cw_ref.shape[1]
    xbc_dim = sw_ref.shape[1]
    inner = SSD_HEADS * SSD_HEAD_DIM
    gn = SSD_GROUPS * SSD_STATE
    gw = inner // SSD_GROUPS
    lp = max(tl, LANES)
    ch0 = CONV_HIST_ROWS - (CONV_WIDTH - 1)
    xh0 = SSD_HIST_ROWS - (SSD_CONV_WIDTH - 1)

    @pl.when(c == 0)
    def _():
        apad[0:ch0, :] = jnp.zeros((ch0, conv_ch), F32)
        apad[ch0:CONV_HIST_ROWS, :] = cbuf_ref[...]
        xpad[0:xh0, :] = jnp.zeros((xh0, xbc_dim), F32)
        xpad[xh0:SSD_HIST_ROWS, :] = xbuf_ref[...]
        st[...] = ssm_ref[...].T

    @pl.when(s == 0)
    def _():
        h1s[1] = jnp.zeros(h1s.shape[1:], F32)

    h = h1s[(s + 1) % 2]
    o_ref[...] = h
    hn_s[...] = _rms(h, mixg_ref[...]).astype(BF16)
    o1, o2, o3 = conv_ch, 2 * conv_ch, 2 * conv_ch + inner
    o4 = o3 + xbc_dim

    xn_s[...] = _rms(x_ref[...], f1g_ref[...]).astype(BF16)
    late = {}

    def independent_dots():
        xpad[SSD_HIST_ROWS:, :] = _dot(hn_s[...], win_ref[:, o3:o4])
        yield
        late["dtr"] = _dot(hn_s[...], win_ref[:, o4:o4 + LANES])
        for i, cs in enumerate(_ff_chunks(f1wg_ref.shape[1])):
            g = _dot(xn_s[...], f1wg_ref[:, cs])
            yield
            hm = (g * jax.nn.sigmoid(g) * _dot(xn_s[...], f1wu_ref[:, cs])).astype(BF16)
            yield
            part = _dot(hm, f1wd_ref[cs, :])
            facc[...] = part if i == 0 else facc[...] + part
            yield
        late["z"] = _dot(hn_s[...], win_ref[:, o2:o3])
        yield

    mxu_todo = independent_dots()

    def mxu_fill():
        next(mxu_todo, None)

    for wb in range(conv_ch // MXU_DIM):
        ws = slice(wb * MXU_DIM, (wb + 1) * MXU_DIM)
        gate = _dot(hn_s[...], win_ref[:, o1 + wb * MXU_DIM:o1 + (wb + 1) * MXU_DIM])
        apad[CONV_HIST_ROWS:, ws] = _dot(hn_s[...], win_ref[:, ws]) * jax.nn.sigmoid(gate)
        for cbk in range(wb * MXU_DIM // LANES, (wb + 1) * MXU_DIM // LANES):
            mxu_fill()
            cs = slice(cbk * LANES, (cbk + 1) * LANES)
            _causal_taps(apad, cs, CONV_HIST_ROWS, cw_ref, cb_ref, CONV_WIDTH, tl, rolls.at[cbk % 2], cout)
    conv = cout[...]
    mu = jnp.mean(conv, axis=-1, keepdims=True)
    xc = conv - mu
    cn = xc * lax.rsqrt(jnp.mean(xc * xc, axis=-1, keepdims=True) + EPS) * lng_ref[...] + lnb_ref[...]
    cact = cn * jax.nn.sigmoid(cn)
    mxu_fill()

    for cbk in range(xbc_dim // LANES):
        cs = slice(cbk * LANES, (cbk + 1) * LANES)
        _causal_taps(xpad, cs, SSD_HIST_ROWS, sw_ref, sb_ref, SSD_CONV_WIDTH, tl, rolls.at[cbk % 2], xbs,
                     act=lambda v: v * jax.nn.sigmoid(v))
    if lp > tl:
        xbs[tl:lp, :] = jnp.zeros((lp - tl, xbc_dim), F32)
    mxu_fill()

    lane = lax.broadcasted_iota(jnp.int32, (1, LANES), 1)
    head_lane = lane < SSD_HEADS
    dtr = late["dtr"] + dtb_ref[...]
    dt = jnp.maximum(dtr, 0.0) + jnp.log1p(jnp.exp(-jnp.abs(dtr)))
    dt = jnp.where(head_lane, dt, 0.0)
    a_neg = jnp.where(head_lane, -jnp.exp(alog_ref[...]), 0.0)
    if lp > tl:
        dt = jnp.concatenate([dt, jnp.zeros((lp - tl, LANES), F32)], axis=0)

    row_i = lax.broadcasted_iota(jnp.int32, (lp, lp), 0)
    col_i = lax.broadcasted_iota(jnp.int32, (lp, lp), 1)
    causal = row_i >= col_i
    tril = jnp.where(causal, 1.0, 0.0).astype(BF16)

    d_hi, d_mid, d_lo = _split3(dt * a_neg)
    acum = _dot(tril, d_hi) + _dot(tril, d_mid) + _dot(tril, d_lo)
    alast = acum[lp - 1:lp, :]
    acum_t = acum.T
    dt_t = dt.T
    ea = jnp.exp(acum)
    wld = jnp.exp(alast - acum) * dt
    eal = jnp.broadcast_to(jnp.exp(alast), (SUBLANES, LANES))

    def expand(v):
        hi, lo = _split2(v)
        return _dot(hi, hexp_ref[...]) + _dot(lo, hexp_ref[...])

    ea_x = expand(ea)
    wld_x = expand(wld)
    eal_x = expand(eal)[0:1, :]
    mxu_fill()

    half = lane < SSD_HEAD_DIM
    pairs_per_group = gw // LANES
    for g in range(SSD_GROUPS):
        gs = slice(g * gw, (g + 1) * gw)
        bs = slice(inner + g * SSD_STATE, inner + (g + 1) * SSD_STATE)
        b_t16 = xbs[:, bs].T.astype(BF16)
        c16 = xbs[:, gn + bs.start:gn + bs.stop].astype(BF16)
        cb = _dot(c16, b_t16)
        st_g = st[:, gs]
        y_inter = _dot(c16, st_g.astype(BF16))
        st[:, gs] = st_g * eal_x[:, gs] + _dot(b_t16, (wld_x[:, gs] * xbs[:, gs]).astype(BF16))
        for jp in range(pairs_per_group):
            pair = g * pairs_per_group + jp
            ls = slice(pair * LANES, (pair + 1) * LANES)
            xs16 = xbs[:, ls].astype(BF16)
            ys = []
            for j in (2 * pair, 2 * pair + 1):
                diff = acum[:, j:j + 1] - acum_t[j:j + 1, :]
                mj = cb * jnp.exp(jnp.where(causal, diff, NEG_BIG)) * dt_t[j:j + 1, :]
                ys.append(_dot(mj.astype(BF16), xs16))
            y_intra = jnp.where(half, ys[0], ys[1])
            ybuf[:, ls] = y_intra + y_inter[:, jp * LANES:(jp + 1) * LANES] * ea_x[:, ls]
            mxu_fill()
    for _ in mxu_todo:
        pass
    y = ybuf[0:tl, :] + dexp_ref[...] * xbs[0:tl, 0:inner]
    z = late["z"]
    y = _rms(y * (z * jax.nn.sigmoid(z)), sng_ref[...])

    o_ref[...] += _dot(cact.astype(BF16), wout_ref[0:conv_ch, :]) + _dot(y.astype(BF16), wout_ref[conv_ch:, :])

    apad[0:CONV_HIST_ROWS, :] = apad[tl:tl + CONV_HIST_ROWS, :]
    xpad[0:SSD_HIST_ROWS, :] = xpad[tl:tl + SSD_HIST_ROWS, :]

    h1s[s % 2] = x_ref[...] + 0.5 * facc[...]

    @pl.when(c == chunks_per_seq - 1)
    def _():
        ncb_ref[...] = apad[ch0:CONV_HIST_ROWS, :]
        nxb_ref[...] = xpad[xh0:SSD_HIST_ROWS, :]
        nss_ref[...] = st[...].T


def _ffn1_mixer(x, conv_buf, xbc_buf, ssm, weights):
    b, t, d = x.shape
    tl = 256 if t % 256 == 0 else t
    assert tl % SUBLANES == 0 and tl >= CONV_HIST_ROWS
    lp = max(tl, LANES)
    cps = t // tl
    n = b * cps
    conv_ch = conv_buf.shape[-1]
    xbc_dim = xbc_buf.shape[-1]
    inner = SSD_HEADS * SSD_HEAD_DIM
    ssm2 = ssm.reshape(b, inner, SSD_STATE)
    mix_chunk = lambda s: jnp.maximum(s - 1, 0)
    seq = lambda rows, cols: pl.BlockSpec((None, rows, cols), lambda s: (mix_chunk(s) // cps, 0, 0))
    out = pl.pallas_call(
        functools.partial(_ffn1_mixer_body, tl=tl, chunks_per_seq=cps),
        grid=(n + 1,),
        in_specs=[pl.BlockSpec((tl, d), lambda s: (jnp.minimum(s, n - 1), 0)),
                  seq(CONV_WIDTH - 1, conv_ch), seq(SSD_CONV_WIDTH - 1, xbc_dim), seq(inner, SSD_STATE)]
                 + [_resident(w.shape) for w in weights],
        out_specs=[pl.BlockSpec((tl, d), lambda s: (mix_chunk(s), 0)),
                   seq(CONV_WIDTH - 1, conv_ch), seq(SSD_CONV_WIDTH - 1, xbc_dim), seq(inner, SSD_STATE)],
        out_shape=[jax.ShapeDtypeStruct((b * t, d), F32),
                   jax.ShapeDtypeStruct(conv_buf.shape, F32),
                   jax.ShapeDtypeStruct(xbc_buf.shape, F32),
                   jax.ShapeDtypeStruct(ssm2.shape, F32)],
        scratch_shapes=[pltpu.VMEM((2, tl, d), F32),
                        pltpu.VMEM((CONV_HIST_ROWS + tl, conv_ch), F32),
                        pltpu.VMEM((SSD_HIST_ROWS + tl, xbc_dim), F32),
                        pltpu.VMEM((SSD_STATE, inner), F32),
                        pltpu.VMEM((tl, d), BF16),
                        pltpu.VMEM((tl, d), BF16),
                        pltpu.VMEM((2, SUBLANES - 1, CONV_HIST_ROWS + tl, LANES), F32),
                        pltpu.VMEM((tl, conv_ch), F32),
                        pltpu.VMEM((lp, xbc_dim), F32),
                        pltpu.VMEM((tl, d), F32),
                        pltpu.VMEM((lp, inner), F32)],
        compiler_params=pltpu.CompilerParams(dimension_semantics=("arbitrary",), vmem_limit_bytes=VMEM_LIMIT_BYTES),
        name="ffn1_mixer",
    )(x.reshape(b * t, d), conv_buf, xbc_buf, ssm2, *weights)
    h2, ncb, nxb, nss = out
    return h2, ncb, nxb, nss.reshape(ssm.shape)


def _row(v):
    return v.reshape(1, -1).astype(F32)


def _pad_lanes(v, width):
    return jnp.pad(v, ((0, 0), (0, width - v.shape[1])))


def kernel(x_prompt, x_sample, p_prompt, p_sample, state_conv, state_ssd_conv, state_ssd,
           ffn1_norm, ffn1_w_gate, ffn1_w_up, ffn1_w_down, mix_norm, w_in,
           conv_dw_w, conv_dw_b, conv_ln_g, conv_ln_b, ssd_conv_w, ssd_conv_b,
           ssd_dt_bias, ssd_A_log, ssd_D, ssd_norm, w_out,
           ffn2_norm, ffn2_w_gate, ffn2_w_up, ffn2_w_down,
           ple_norm, ple_w_proj, ple_w_gate, final_norm):
    depth = ffn1_norm.shape[0]
    assert depth == 1, "the FFN2 kernel fuses the final norm, so it must be the last layer's"
    i = 0
    bp, sp, d = x_prompt.shape
    conv_ch = state_conv.shape[-1]
    xbc_dim = state_ssd_conv.shape[-1]
    head_expand = (jnp.arange(LANES)[:, None] == jnp.arange(SSD_HEADS * SSD_HEAD_DIM)[None, :] // SSD_HEAD_DIM).astype(BF16)

    win = _pad_lanes(w_in[i], w_in.shape[2] - SSD_HEADS + LANES).astype(BF16)
    mix_w = (_row(mix_norm[i]), win, conv_dw_w[i], _row(conv_dw_b[i]), _row(conv_ln_g[i]), _row(conv_ln_b[i]),
             ssd_conv_w[i], _row(ssd_conv_b[i]), _pad_lanes(_row(ssd_dt_bias[i]), LANES),
             _pad_lanes(_row(ssd_A_log[i]), LANES), _row(jnp.repeat(ssd_D[i], SSD_HEAD_DIM)),
             _row(ssd_norm[i]), head_expand, w_out[i].astype(BF16))
    f1_w = (_row(ffn1_norm[i]), ffn1_w_gate[i].astype(BF16), ffn1_w_up[i].astype(BF16), ffn1_w_down[i].astype(BF16))
    f2_w = (_row(ffn2_norm[i]), ffn2_w_gate[i].astype(BF16), ffn2_w_up[i].astype(BF16), ffn2_w_down[i].astype(BF16),
            _row(ple_norm[i]), ple_w_proj[i].astype(BF16), ple_w_gate[i].astype(BF16), _row(final_norm))
    zero_hist = (jnp.zeros((bp, CONV_WIDTH - 1, conv_ch), F32), jnp.zeros((bp, SSD_CONV_WIDTH - 1, xbc_dim), F32),
                 jnp.zeros((bp,) + state_ssd.shape[2:], F32))
    outs = []
    for x, p, hist in ((x_prompt, p_prompt[i], zero_hist),
                       (x_sample, p_sample[i], (state_conv[i], state_ssd_conv[i], state_ssd[i]))):
        b, t, _ = x.shape
        h2, ncb, nxb, nss = _ffn1_mixer(x, *hist, f1_w + mix_w)
        y = _ffn2(h2, p.reshape(b * t, -1), *f2_w).reshape(b, t, d)
        outs.append((y, ncb[None], nxb[None], nss[None]))
    (y_p, c1, x1, s1), (y_s, c2, x2, s2) = outs
    return (y_p, y_s, c1, x1, s1, c2, x2, s2)
```

```python
import functools

import jax
import jax.numpy as jnp
from jax import lax
from jax.experimental import pallas as pl
from jax.experimental.pallas import tpu as pltpu

EPS = 1e-6
CONV_WIDTH = 31
SSD_CONV_WIDTH = 4
SSD_HEADS = 16
SSD_HEAD_DIM = 64
SSD_GROUPS = 2
SSD_STATE = 128
LANES = 128
SUBLANES = 8
MXU_DIM = 256
VMEM_LIMIT_BYTES = 56 * 1024 * 1024
CONV_HIST_ROWS = 32
SSD_HIST_ROWS = 8
NEG_BIG = -1e30
TAP_ROWS = 128

F32 = jnp.float32
BF16 = jnp.bfloat16


def _rms(x, g):
    ms = jnp.mean(x * x, axis=-1, keepdims=True)
    return x * lax.rsqrt(ms + EPS) * g


def _dot(a, b):
    return jnp.dot(a, b, preferred_element_type=F32)


def _split2(v):
    hi = v.astype(BF16)
    lo = (v - hi.astype(F32)).astype(BF16)
    return hi, lo


def _split3(v):
    hi = v.astype(BF16)
    r = v - hi.astype(F32)
    mid = r.astype(BF16)
    lo = (r - mid.astype(F32)).astype(BF16)
    return hi, mid, lo


def _causal_taps(ext_ref, cs, hist_rows, w_ref, b_ref, n_taps, tl, roll_ref, out_ref, act=None):
    rows = hist_rows + tl
    ext = ext_ref[:, cs]
    taps = []
    for k in range(n_taps):
        off = hist_rows - (n_taps - 1) + k
        r = (-off) % SUBLANES
        taps.append((k, r, off + r))
    rots = {0: ext.reshape(rows // SUBLANES, SUBLANES, LANES)}

    def tile_rotation(q):
        if q not in rots:
            rots[q] = pltpu.roll(tile_rotation(q - (q & -q)), q & -q, axis=1)
        return rots[q]

    sub = lax.broadcasted_iota(jnp.int32, (1, SUBLANES, LANES), 1)
    for r in sorted({r for _, r, _ in taps} - {0}):
        rot = tile_rotation(r)
        above = jnp.concatenate([rot[-1:], rot[:-1]], axis=0)
        roll_ref[r - 1, 0:rows, :] = jnp.where(sub >= r, rot, above).reshape(rows, LANES)
    for t0 in range(0, tl, TAP_ROWS):
        nr = min(TAP_ROWS, tl - t0)
        acc = jnp.broadcast_to(b_ref[:, cs], (nr, LANES))
        for k, r, start in taps:
            lo = start + t0
            win = ext_ref[lo:lo + nr, cs] if r == 0 else roll_ref[r - 1, lo:lo + nr, :]
            acc = acc + w_ref[k:k + 1, cs] * win
        out_ref[t0:t0 + nr, cs] = acc if act is None else act(acc)


def _ff_chunks(d_ff, n_chunks=4):
    tiles = d_ff // MXU_DIM
    assert tiles * MXU_DIM == d_ff and tiles >= n_chunks
    edges = [MXU_DIM * ((tiles * c + n_chunks - 1) // n_chunks) for c in range(n_chunks + 1)]
    return tuple(slice(lo, hi) for lo, hi in zip(edges[:-1], edges[1:]))


def _swiglu(xn, wg_ref, wu_ref, wd_ref):
    acc = None
    for cs in _ff_chunks(wg_ref.shape[1]):
        g = _dot(xn, wg_ref[:, cs])
        u = _dot(xn, wu_ref[:, cs])
        hm = (g * jax.nn.sigmoid(g) * u).astype(BF16)
        part = _dot(hm, wd_ref[cs, :])
        acc = part if acc is None else acc + part
    return acc


def _ffn1_body(x_ref, ng_ref, wg_ref, wu_ref, wd_ref, o_ref):
    x = x_ref[...]
    xn = _rms(x, ng_ref[...]).astype(BF16)
    o_ref[...] = x + 0.5 * _swiglu(xn, wg_ref, wu_ref, wd_ref)


def _ffn2_body(x_ref, p_ref, ng_ref, wg_ref, wu_ref, wd_ref, pn_ref, pwp_ref, pwg_ref, fn_ref, o_ref):
    x = x_ref[...]
    ple = _dot(p_ref[...].astype(BF16), pwp_ref[...])
    xn = _rms(x, ng_ref[...]).astype(BF16)
    h = x + 0.5 * _swiglu(xn, wg_ref, wu_ref, wd_ref)
    gate = jax.nn.sigmoid(_dot(_rms(h, pn_ref[...]).astype(BF16), pwg_ref[...]))
    h = h + gate * ple
    o_ref[...] = _rms(h, fn_ref[...])


def _resident(shape):
    nd = len(shape)
    return pl.BlockSpec(shape, lambda *_: (0,) * nd, pipeline_mode=pl.Buffered(1))


def _token_tile(n_tokens):
    for tm in (1024, 512, 256, 128, 64, 32, 16, 8):
        if n_tokens % tm == 0:
            return tm
    raise ValueError(f"token count {n_tokens} must be a multiple of {SUBLANES}")


def _ffn2(x, p, ng, wg, wu, wd, pn, pwp, pwg, fn):
    n, d = x.shape
    tm = _token_tile(n)
    row = pl.BlockSpec((tm, d), lambda i: (i, 0))
    prow = pl.BlockSpec((tm, p.shape[1]), lambda i: (i, 0))
    weights = (ng, wg, wu, wd, pn, pwp, pwg, fn)
    return pl.pallas_call(
        _ffn2_body,
        grid=(n // tm,),
        in_specs=[row, prow] + [_resident(w.shape) for w in weights],
        out_specs=row,
        out_shape=jax.ShapeDtypeStruct((n, d), F32),
        compiler_params=pltpu.CompilerParams(dimension_semantics=("arbitrary",), vmem_limit_bytes=VMEM_LIMIT_BYTES),
        name="ffn2",
    )(x, p, *weights)


def _mixer_body(*refs, tl, chunks_per_seq, fuse_ffn1):
    refs = list(refs)
    x_ref, cbuf_ref, xbuf_ref, ssm_ref = refs[:4]
    del refs[:4]
    if fuse_ffn1:
        f1g_ref, f1wg_ref, f1wu_ref, f1wd_ref = refs[:4]
        del refs[:4]
    (mixg_ref, win_ref, wdt_ref, cw_ref, cb_ref, lng_ref, lnb_ref, sw_ref, sb_ref, dtb_ref, alog_ref, dexp_ref,
     sng_ref, hexp_ref, wout_ref, o_ref, ncb_ref, nxb_ref, nss_ref,
     h1s, apad, xpad, st, hn_s, xn_s, rolls, cout, xbs, facc, ybuf) = refs

    s = pl.program_id(0)
    mix_chunk = jnp.maximum(s - 1, 0) if fuse_ffn1 else s
    c = mix_chunk % chunks_per_seq
    conv_ch = cw_ref.shape[1]
    xbc_dim = sw_ref.shape[1]
    inner = SSD_HEADS * SSD_HEAD_DIM
    gn = SSD_GROUPS * SSD_STATE
    gw = inner // SSD_GROUPS
    lp = max(tl, LANES)
    ch0 = CONV_HIST_ROWS - (CONV_WIDTH - 1)
    xh0 = SSD_HIST_ROWS - (SSD_CONV_WIDTH - 1)

    @pl.when(c == 0)
    def _():
        apad[0:ch0, :] = jnp.zeros((ch0, conv_ch), F32)
        apad[ch0:CONV_HIST_ROWS, :] = cbuf_ref[...]
        xpad[0:xh0, :] = jnp.zeros((xh0, xbc_dim), F32)
        xpad[xh0:SSD_HIST_ROWS, :] = xbuf_ref[...]
        st[...] = ssm_ref[...].T

    if fuse_ffn1:
        @pl.when(s == 0)
        def _():
            h1s[1] = jnp.zeros(h1s.shape[1:], F32)

        h = h1s[(s + 1) % 2]
    else:
        h = x_ref[...]
    o_ref[...] = h
    hn_s[...] = _rms(h, mixg_ref[...]).astype(BF16)
    o1, o2, o3 = conv_ch, 2 * conv_ch, 2 * conv_ch + inner
    o4 = o3 + xbc_dim

    late = {}
    if fuse_ffn1:
        xn_s[...] = _rms(x_ref[...], f1g_ref[...]).astype(BF16)

    def independent_dots():
        if fuse_ffn1:
            for i, cs in enumerate(_ff_chunks(f1wg_ref.shape[1])):
                g = _dot(xn_s[...], f1wg_ref[:, cs])
                hm = (g * jax.nn.sigmoid(g) * _dot(xn_s[...], f1wu_ref[:, cs])).astype(BF16)
                yield
                part = _dot(hm, f1wd_ref[cs, :])
                facc[...] = part if i == 0 else facc[...] + part
                yield
        late["z"] = _dot(hn_s[...],win_ref[:, o2:o3])
        yield

    mxu_todo = independent_dots()

    def mxu_fill():
        next(mxu_todo, None)

    xpad[SSD_HIST_ROWS:, :] = _dot(hn_s[...],win_ref[:, o3:o4])
    dtr = _dot(hn_s[...],wdt_ref[...]) + dtb_ref[...]
    for wb in range(conv_ch // MXU_DIM):
        ws = slice(wb * MXU_DIM, (wb + 1) * MXU_DIM)
        gate = _dot(hn_s[...],win_ref[:, o1 + wb * MXU_DIM:o1 + (wb + 1) * MXU_DIM])
        apad[CONV_HIST_ROWS:, ws] = _dot(hn_s[...],win_ref[:, ws]) * jax.nn.sigmoid(gate)

    for cbk in range(xbc_dim // LANES):
        cs = slice(cbk * LANES, (cbk + 1) * LANES)
        _causal_taps(xpad, cs, SSD_HIST_ROWS, sw_ref, sb_ref, SSD_CONV_WIDTH, tl, rolls.at[cbk % 2], xbs,
                     act=lambda v: v * jax.nn.sigmoid(v))
    if lp > tl:
        xbs[tl:lp, :] = jnp.zeros((lp - tl, xbc_dim), F32)
    mxu_fill()

    lane = lax.broadcasted_iota(jnp.int32, (1, LANES), 1)
    head_lane = lane < SSD_HEADS
    dt =jnp.maximum(dtr, 0.0) + jnp.log1p(jnp.exp(-jnp.abs(dtr)))
    dt = jnp.where(head_lane, dt, 0.0)
    a_neg = jnp.where(head_lane, -jnp.exp(alog_ref[...]), 0.0)
    if lp > tl:
        dt = jnp.concatenate([dt, jnp.zeros((lp - tl, LANES), F32)], axis=0)

    row_i = lax.broadcasted_iota(jnp.int32, (lp, lp), 0)
    col_i = lax.broadcasted_iota(jnp.int32, (lp, lp), 1)
    causal = row_i >= col_i
    tril = jnp.where(causal, 1.0, 0.0).astype(BF16)

    csum = _dot(tril, jnp.concatenate(_split3(dt * a_neg), axis=1))
    acum = csum[:, 0:LANES] + csum[:, LANES:2 * LANES] + csum[:, 2 * LANES:]
    alast = acum[lp - 1:lp, :]
    acum_t = acum.T
    dt_t = dt.T
    ea = jnp.exp(acum)
    wld = jnp.exp(alast - acum) * dt
    eal = jnp.broadcast_to(jnp.exp(alast), (SUBLANES, LANES))

    def expand(v):
        hi, lo = _split2(v)
        return _dot(jnp.concatenate([hi, lo], axis=1), hexp_ref[...])

    ea_x = expand(ea)
    wld_x = expand(wld)
    eal_x = expand(eal)[0:1, :]
    mxu_fill()

    half = lane < SSD_HEAD_DIM
    pairs_per_group = gw // LANES
    for g in range(SSD_GROUPS):
        gs = slice(g * gw, (g + 1) * gw)
        bs = slice(inner + g * SSD_STATE, inner + (g + 1) * SSD_STATE)
        b_t16 = xbs[:, bs].T.astype(BF16)
        c16 = xbs[:, gn + bs.start:gn + bs.stop].astype(BF16)
        cb = _dot(c16, b_t16)
        st_g = st[:, gs]
        y_inter = _dot(c16, st_g.astype(BF16))
        st[:, gs] = st_g * eal_x[:, gs] + _dot(b_t16, (wld_x[:, gs] * xbs[:, gs]).astype(BF16))
        for jp in range(pairs_per_group):
            pair = g * pairs_per_group + jp
            ls = slice(pair * LANES, (pair + 1) * LANES)
            xs16 = xbs[:, ls].astype(BF16)
            ys = []
            for j in (2 * pair, 2 * pair + 1):
                diff = acum[:, j:j + 1] - acum_t[j:j + 1, :]
                mj = cb * jnp.exp(jnp.where(causal, diff, NEG_BIG)) * dt_t[j:j + 1, :]
                ys.append(_dot(mj.astype(BF16), xs16))
            y_intra = jnp.where(half, ys[0], ys[1])
            ybuf[:, ls] = y_intra + y_inter[:, jp * LANES:(jp + 1) * LANES] * ea_x[:, ls]
            if jp % 2 == 1:
                mxu_fill()

    for cbk in range(conv_ch // LANES):
        mxu_fill()
        cs = slice(cbk * LANES, (cbk + 1) * LANES)
        _causal_taps(apad, cs, CONV_HIST_ROWS, cw_ref, cb_ref, CONV_WIDTH, tl, rolls.at[cbk % 2], cout)
    conv = cout[...]
    mu = jnp.mean(conv, axis=-1, keepdims=True)
    xc = conv - mu
    cn = xc * lax.rsqrt(jnp.mean(xc * xc, axis=-1, keepdims=True) + EPS) * lng_ref[...] + lnb_ref[...]
    cact = cn * jax.nn.sigmoid(cn)
    for _ in mxu_todo:
        pass

    y = ybuf[0:tl, :] + dexp_ref[...] * xbs[0:tl, 0:inner]
    z = late["z"]
    y = _rms(y * (z * jax.nn.sigmoid(z)), sng_ref[...])

    o_ref[...] += _dot(cact.astype(BF16), wout_ref[0:conv_ch, :]) + _dot(y.astype(BF16), wout_ref[conv_ch:, :])

    apad[0:CONV_HIST_ROWS, :] = apad[tl:tl + CONV_HIST_ROWS, :]
    xpad[0:SSD_HIST_ROWS, :] = xpad[tl:tl + SSD_HIST_ROWS, :]

    if fuse_ffn1:
        h1s[s % 2] = x_ref[...] + 0.5 * facc[...]

    @pl.when(c == chunks_per_seq - 1)
    def _():
        ncb_ref[...] = apad[ch0:CONV_HIST_ROWS, :]
        nxb_ref[...] = xpad[xh0:SSD_HIST_ROWS, :]
        nss_ref[...] = st[...].T


def _ffn1(x, ng, wg, wu, wd):
    n, d = x.shape
    tm = _token_tile(n)
    row = pl.BlockSpec((tm, d), lambda i: (i, 0))
    weights = (ng, wg, wu, wd)
    return pl.pallas_call(
        _ffn1_body,
        grid=(n // tm,),
        in_specs=[row] + [_resident(w.shape) for w in weights],
        out_specs=row,
        out_shape=jax.ShapeDtypeStruct((n, d), F32),
        compiler_params=pltpu.CompilerParams(dimension_semantics=("arbitrary",), vmem_limit_bytes=VMEM_LIMIT_BYTES),
        name="ffn1",
    )(x, *weights)


def _ffn1_mixer(x, conv_buf, xbc_buf, ssm, f1_weights, mix_weights):
    b, t, d = x.shape
    tl = 256 if t % 256 == 0 else t
    assert tl % SUBLANES == 0 and tl >= CONV_HIST_ROWS
    lp = max(tl, LANES)
    cps = t // tl
    n = b * cps
    fuse_ffn1 = cps > 1
    x = x.reshape(b * t, d)
    w_specs = [_resident(w.shape) for w in mix_weights]
    win = mix_weights[1]
    w_specs[1] = _resident((win.shape[0], win.shape[1] // LANES * LANES))
    if fuse_ffn1:
        w_specs = [_resident(w.shape) for w in f1_weights] + w_specs
        weights = tuple(f1_weights) + tuple(mix_weights)
        steps = n + 1
        mix_chunk = lambda s: jnp.maximum(s - 1, 0)
        x_chunk = lambda s: jnp.minimum(s, n - 1)
    else:
        x = _ffn1(x, *f1_weights)
        weights = tuple(mix_weights)
        steps = n
        mix_chunk = x_chunk = lambda s: s
    conv_ch = conv_buf.shape[-1]
    xbc_dim = xbc_buf.shape[-1]
    inner = SSD_HEADS * SSD_HEAD_DIM
    ssm2 = ssm.reshape(b, inner, SSD_STATE)
    seq = lambda rows, cols: pl.BlockSpec((None, rows, cols), lambda s: (mix_chunk(s) // cps, 0, 0))
    out = pl.pallas_call(
        functools.partial(_mixer_body, tl=tl, chunks_per_seq=cps, fuse_ffn1=fuse_ffn1),
        grid=(steps,),
        in_specs=[pl.BlockSpec((tl, d), lambda s: (x_chunk(s), 0)),
                  seq(CONV_WIDTH - 1, conv_ch), seq(SSD_CONV_WIDTH - 1, xbc_dim), seq(inner, SSD_STATE)] + w_specs,
        out_specs=[pl.BlockSpec((tl, d), lambda s: (mix_chunk(s), 0)),
                   seq(CONV_WIDTH - 1, conv_ch), seq(SSD_CONV_WIDTH - 1, xbc_dim), seq(inner, SSD_STATE)],
        out_shape=[jax.ShapeDtypeStruct((b * t, d), F32),
                   jax.ShapeDtypeStruct(conv_buf.shape, F32),
                   jax.ShapeDtypeStruct(xbc_buf.shape, F32),
                   jax.ShapeDtypeStruct(ssm2.shape, F32)],
        scratch_shapes=[pltpu.VMEM((2, tl, d), F32),
                        pltpu.VMEM((CONV_HIST_ROWS + tl, conv_ch), F32),
                        pltpu.VMEM((SSD_HIST_ROWS + tl, xbc_dim), F32),
                        pltpu.VMEM((SSD_STATE, inner), F32),
                        pltpu.VMEM((tl, d), BF16),
                        pltpu.VMEM((tl, d), BF16),
                        pltpu.VMEM((2, SUBLANES - 1, CONV_HIST_ROWS + tl, LANES), F32),
                        pltpu.VMEM((tl, conv_ch), F32),
                        pltpu.VMEM((lp, xbc_dim), F32),
                        pltpu.VMEM((tl, d), F32),
                        pltpu.VMEM((lp, inner), F32)],
        compiler_params=pltpu.CompilerParams(dimension_semantics=("arbitrary",), vmem_limit_bytes=VMEM_LIMIT_BYTES),
        name="ffn1_mixer" if fuse_ffn1 else "mixer",
    )(x, conv_buf, xbc_buf, ssm2, *weights)
    h2, ncb, nxb, nss = out
    return h2, ncb, nxb, nss.reshape(ssm.shape)


def _row(v):
    return v.reshape(1, -1).astype(F32)


def _pad_lanes(v, width):
    return jnp.pad(v, ((0, 0), (0, width - v.shape[1])))


def kernel(x_prompt, x_sample, p_prompt, p_sample, state_conv, state_ssd_conv, state_ssd,
           ffn1_norm, ffn1_w_gate, ffn1_w_up, ffn1_w_down, mix_norm, w_in,
           conv_dw_w, conv_dw_b, conv_ln_g, conv_ln_b, ssd_conv_w, ssd_conv_b,
           ssd_dt_bias, ssd_A_log, ssd_D, ssd_norm, w_out,
           ffn2_norm, ffn2_w_gate, ffn2_w_up, ffn2_w_down,
           ple_norm, ple_w_proj, ple_w_gate, final_norm):
    depth = ffn1_norm.shape[0]
    assert depth == 1, "the FFN2 kernel fuses the final norm, so it must be the last layer's"
    i = 0
    bp, sp, d = x_prompt.shape
    conv_ch = state_conv.shape[-1]
    xbc_dim = state_ssd_conv.shape[-1]
    head_expand = (jnp.arange(LANES)[:, None] == jnp.arange(SSD_HEADS * SSD_HEAD_DIM)[None, :] // SSD_HEAD_DIM).astype(BF16)
    head_expand = jnp.concatenate([head_expand, head_expand], axis=0)

    n_dt = w_in.shape[2] - SSD_HEADS
    win = w_in[i].astype(BF16)
    wdt =_pad_lanes(w_in[i][:, n_dt:], LANES).astype(BF16)
    mix_w = (_row(mix_norm[i]), win, wdt, conv_dw_w[i], _row(conv_dw_b[i]), _row(conv_ln_g[i]), _row(conv_ln_b[i]),
             ssd_conv_w[i], _row(ssd_conv_b[i]), _pad_lanes(_row(ssd_dt_bias[i]), LANES),
             _pad_lanes(_row(ssd_A_log[i]), LANES), _row(jnp.repeat(ssd_D[i], SSD_HEAD_DIM)),
             _row(ssd_norm[i]), head_expand, w_out[i].astype(BF16))
    f1_w = (_row(ffn1_norm[i]), ffn1_w_gate[i].astype(BF16), ffn1_w_up[i].astype(BF16), ffn1_w_down[i].astype(BF16))
    f2_w = (_row(ffn2_norm[i]), ffn2_w_gate[i].astype(BF16), ffn2_w_up[i].astype(BF16), ffn2_w_down[i].astype(BF16),
            _row(ple_norm[i]), ple_w_proj[i].astype(BF16), ple_w_gate[i].astype(BF16), _row(final_norm))
    zero_hist = (jnp.zeros((bp, CONV_WIDTH - 1, conv_ch), F32), jnp.zeros((bp, SSD_CONV_WIDTH - 1, xbc_dim), F32),
                 jnp.zeros((bp,) + state_ssd.shape[2:], F32))
    outs = []
    for x, p, hist in ((x_prompt, p_prompt[i], zero_hist),
                       (x_sample, p_sample[i], (state_conv[i], state_ssd_conv[i], state_ssd[i]))):
        b, t, _ = x.shape
        h2, ncb, nxb, nss = _ffn1_mixer(x, *hist, f1_w, mix_w)
        y = _ffn2(h2, p.reshape(b * t, -1), *f2_w).reshape(b, t, d)
        outs.append((y, ncb[None], nxb[None], nss[None]))
    (y_p, c1, x1, s1), (y_s, c2, x2, s2) = outs
    return (y_p, y_s, c1, x1, s1, c2, x2, s2)
```

```python
import functools

import jax
import jax.numpy as jnp
from jax import lax
from jax.experimental import pallas as pl
from jax.experimental.pallas import tpu as pltpu

EPS = 1e-6
CONV_WIDTH = 31
SSD_CONV_WIDTH = 4
SSD_HEADS = 16
SSD_HEAD_DIM = 64
SSD_GROUPS = 2
SSD_STATE = 128
LANES = 128
SUBLANES = 8
MXU_DIM = 256
VMEM_LIMIT_BYTES = 56 * 1024 * 1024
CONV_HIST_ROWS = 32
SSD_HIST_ROWS = 8
NEG_BIG = -1e30
TAP_ROWS = 128

F32 = jnp.float32
BF16 = jnp.bfloat16


def _rms(x, g):
    ms = jnp.mean(x * x, axis=-1, keepdims=True)
    return x * lax.rsqrt(ms + EPS) * g


def _dot(a, b):
    return jnp.dot(a, b, preferred_element_type=F32)


def _split2(v):
    hi = v.astype(BF16)
    lo = (v - hi.astype(F32)).astype(BF16)
    return hi, lo


def _split3(v):
    hi = v.astype(BF16)
    r = v - hi.astype(F32)
    mid = r.astype(BF16)
    lo = (r - mid.astype(F32)).astype(BF16)
    return hi, mid, lo


def _causal_taps(ext_ref, cs, hist_rows, w_ref, b_ref, n_taps, tl, roll_ref, out_ref, act=None):
    rows = hist_rows + tl
    ext = ext_ref[:, cs]
    taps = []
    for k in range(n_taps):
        off = hist_rows - (n_taps - 1) + k
        r = (-off) % SUBLANES
        taps.append((k, r, off + r))
    rots = {0: ext.reshape(rows // SUBLANES, SUBLANES, LANES)}

    def tile_rotation(q):
        if q not in rots:
            rots[q] = pltpu.roll(tile_rotation(q - (q & -q)), q & -q, axis=1)
        return rots[q]

    sub = lax.broadcasted_iota(jnp.int32, (1, SUBLANES, LANES), 1)
    for r in sorted({r for _, r, _ in taps} - {0}):
        rot = tile_rotation(r)
        above = jnp.concatenate([rot[-1:], rot[:-1]], axis=0)
        roll_ref[r - 1, 0:rows, :] = jnp.where(sub >= r, rot, above).reshape(rows, LANES)
    for t0 in range(0, tl, TAP_ROWS):
        nr = min(TAP_ROWS, tl - t0)
        acc = jnp.broadcast_to(b_ref[:, cs], (nr, LANES))
        for k, r, start in taps:
            lo = start + t0
            win = ext_ref[lo:lo + nr, cs] if r == 0 else roll_ref[r - 1, lo:lo + nr, :]
            acc = acc + w_ref[k:k + 1, cs] * win
        out_ref[t0:t0 + nr, cs] = acc if act is None else act(acc)


def _ff_chunks(d_ff, n_chunks=4):
    tiles = d_ff // MXU_DIM
    assert tiles * MXU_DIM == d_ff and tiles >= n_chunks
    edges = [MXU_DIM * ((tiles * c + n_chunks - 1) // n_chunks) for c in range(n_chunks + 1)]
    return tuple(slice(lo, hi) for lo, hi in zip(edges[:-1], edges[1:]))


def _swiglu(xn, wg_ref, wu_ref, wd_ref):
    acc = None
    for cs in _ff_chunks(wg_ref.shape[1]):
        g = _dot(xn, wg_ref[:, cs])
        u = _dot(xn, wu_ref[:, cs])
        hm = (g * jax.nn.sigmoid(g) * u).astype(BF16)
        part = _dot(hm, wd_ref[cs, :])
        acc = part if acc is None else acc + part
    return acc


def _ffn1_body(x_ref, ng_ref, wg_ref, wu_ref, wd_ref, o_ref):
    x = x_ref[...]
    xn = _rms(x, ng_ref[...]).astype(BF16)
    o_ref[...] = x + 0.5 * _swiglu(xn, wg_ref, wu_ref, wd_ref)


def _ffn2_body(x_ref, p_ref, ng_ref, wg_ref, wu_ref, wd_ref, pn_ref, pwp_ref, pwg_ref, fn_ref, o_ref):
    x = x_ref[...]
    ple = _dot(p_ref[...].astype(BF16), pwp_ref[...])
    xn = _rms(x, ng_ref[...]).astype(BF16)
    h = x + 0.5 * _swiglu(xn, wg_ref, wu_ref, wd_ref)
    gate = jax.nn.sigmoid(_dot(_rms(h, pn_ref[...]).astype(BF16), pwg_ref[...]))
    h = h + gate * ple
    o_ref[...] = _rms(h, fn_ref[...])


def _resident(shape):
    nd = len(shape)
    return pl.BlockSpec(shape, lambda *_: (0,) * nd, pipeline_mode=pl.Buffered(1))


def _token_tile(n_tokens):
    for tm in (1024, 512, 256, 128, 64, 32, 16, 8):
        if n_tokens % tm == 0:
            return tm
    raise ValueError(f"token count {n_tokens} must be a multiple of {SUBLANES}")


def _ffn2(x, p, ng, wg, wu, wd, pn, pwp, pwg, fn):
    n, d = x.shape
    tm = _token_tile(n)
    row = pl.BlockSpec((tm, d), lambda i: (i, 0))
    prow = pl.BlockSpec((tm, p.shape[1]), lambda i: (i, 0))
    weights = (ng, wg, wu, wd, pn, pwp, pwg, fn)
    return pl.pallas_call(
        _ffn2_body,
        grid=(n // tm,),
        in_specs=[row, prow] + [_resident(w.shape) for w in weights],
        out_specs=row,
        out_shape=jax.ShapeDtypeStruct((n, d), F32),
        compiler_params=pltpu.CompilerParams(dimension_semantics=("arbitrary",), vmem_limit_bytes=VMEM_LIMIT_BYTES),
        name="ffn2",
    )(x, p, *weights)


def _mixer_body(*refs, tl, chunks_per_seq, fuse_ffn1):
    refs = list(refs)
    x_ref, cbuf_ref, xbuf_ref, ssm_ref = refs[:4]
    del refs[:4]
    if fuse_ffn1:
        f1g_ref, f1wg_ref, f1wu_ref, f1wd_ref = refs[:4]
        del refs[:4]
    (mixg_ref, win_ref, cw_ref, cb_ref, lng_ref, lnb_ref, sw_ref, sb_ref, dtb_ref, alog_ref, dexp_ref,
     sng_ref, hexp_ref, wout_ref, o_ref, ncb_ref, nxb_ref, nss_ref,
     h1s, apad, xpad, st, hn_s, xn_s, rolls, cout, xbs, facc, ybuf) = refs

    s = pl.program_id(0)
    mix_chunk = jnp.maximum(s - 1, 0) if fuse_ffn1 else s
    c = mix_chunk % chunks_per_seq
    conv_ch = cw_ref.shape[1]
    xbc_dim = sw_ref.shape[1]
    inner = SSD_HEADS * SSD_HEAD_DIM
    gn = SSD_GROUPS * SSD_STATE
    gw = inner // SSD_GROUPS
    lp = max(tl, LANES)
    ch0 = CONV_HIST_ROWS - (CONV_WIDTH - 1)
    xh0 = SSD_HIST_ROWS - (SSD_CONV_WIDTH - 1)

    @pl.when(c == 0)
    def _():
        apad[0:ch0, :] = jnp.zeros((ch0, conv_ch), F32)
        apad[ch0:CONV_HIST_ROWS, :] = cbuf_ref[...]
        xpad[0:xh0, :] = jnp.zeros((xh0, xbc_dim), F32)
        xpad[xh0:SSD_HIST_ROWS, :] = xbuf_ref[...]
        st[...] = ssm_ref[...].T

    if fuse_ffn1:
        @pl.when(s == 0)
        def _():
            h1s[1] = jnp.zeros(h1s.shape[1:], F32)

        h = h1s[(s + 1) % 2]
    else:
        h = x_ref[...]
    o_ref[...] = h
    hn_s[...] = _rms(h, mixg_ref[...]).astype(BF16)
    o1, o2, o3 = conv_ch, 2 * conv_ch, 2 * conv_ch + inner
    o4 = o3 + xbc_dim

    late = {}
    if fuse_ffn1:
        xn_s[...] = _rms(x_ref[...], f1g_ref[...]).astype(BF16)

    def independent_dots():
        if fuse_ffn1:
            for i, cs in enumerate(_ff_chunks(f1wg_ref.shape[1])):
                g = _dot(xn_s[...], f1wg_ref[:, cs])
                hm = (g * jax.nn.sigmoid(g) * _dot(xn_s[...], f1wu_ref[:, cs])).astype(BF16)
                yield
                part = _dot(hm, f1wd_ref[cs, :])
                facc[...] = part if i == 0 else facc[...] + part
                yield
        late["z"] = _dot(hn_s[...],win_ref[:, o2:o3])
        yield

    mxu_todo = independent_dots()

    def mxu_fill():
        next(mxu_todo, None)

    xpad[SSD_HIST_ROWS:, :] = _dot(hn_s[...],win_ref[:, o3:o4])
    dtr = _dot(hn_s[...], win_ref[:, o4:o4 + LANES]) + dtb_ref[...]
    for wb in range(conv_ch // MXU_DIM):
        ws = slice(wb * MXU_DIM, (wb + 1) * MXU_DIM)
        gate = _dot(hn_s[...],win_ref[:, o1 + wb * MXU_DIM:o1 + (wb + 1) * MXU_DIM])
        apad[CONV_HIST_ROWS:, ws] = _dot(hn_s[...],win_ref[:, ws]) * jax.nn.sigmoid(gate)

    for cbk in range(xbc_dim // LANES):
        cs = slice(cbk * LANES, (cbk + 1) * LANES)
        _causal_taps(xpad, cs, SSD_HIST_ROWS, sw_ref, sb_ref, SSD_CONV_WIDTH, tl, rolls.at[cbk % 2], xbs,
                     act=lambda v: v * jax.nn.sigmoid(v))
    if lp > tl:
        xbs[tl:lp, :] = jnp.zeros((lp - tl, xbc_dim), F32)
    mxu_fill()

    lane = lax.broadcasted_iota(jnp.int32, (1, LANES), 1)
    head_lane = lane < SSD_HEADS
    dt =jnp.maximum(dtr, 0.0) + jnp.log1p(jnp.exp(-jnp.abs(dtr)))
    dt = jnp.where(head_lane, dt, 0.0)
    a_neg = jnp.where(head_lane, -jnp.exp(alog_ref[...]), 0.0)
    if lp > tl:
        dt = jnp.concatenate([dt, jnp.zeros((lp - tl, LANES), F32)], axis=0)

    row_i = lax.broadcasted_iota(jnp.int32, (lp, lp), 0)
    col_i = lax.broadcasted_iota(jnp.int32, (lp, lp), 1)
    causal = row_i >= col_i
    tril = jnp.where(causal, 1.0, 0.0).astype(BF16)

    csum = _dot(tril, jnp.concatenate(_split3(dt * a_neg), axis=1))
    acum = csum[:, 0:LANES] + csum[:, LANES:2 * LANES] + csum[:, 2 * LANES:]
    alast = acum[lp - 1:lp, :]
    acum_t = acum.T
    dt_t = dt.T
    ea = jnp.exp(acum)
    wld = jnp.exp(alast - acum) * dt
    eal = jnp.broadcast_to(jnp.exp(alast), (SUBLANES, LANES))

    def expand(v):
        hi, lo = _split2(v)
        return _dot(jnp.concatenate([hi, lo], axis=1), hexp_ref[...])

    ea_x = expand(ea)
    wld_x = expand(wld)
    eal_x = expand(eal)[0:1, :]
    mxu_fill()

    half = lane < SSD_HEAD_DIM
    pairs_per_group = gw // LANES
    for g in range(SSD_GROUPS):
        gs = slice(g * gw, (g + 1) * gw)
        bs = slice(inner + g * SSD_STATE, inner + (g + 1) * SSD_STATE)
        b_t16 = xbs[:, bs].T.astype(BF16)
        c16 = xbs[:, gn + bs.start:gn + bs.stop].astype(BF16)
        cb = _dot(c16, b_t16)
        st_g = st[:, gs]
        y_inter = _dot(c16, st_g.astype(BF16))
        st[:, gs] = st_g * eal_x[:, gs] + _dot(b_t16, (wld_x[:, gs] * xbs[:, gs]).astype(BF16))
        for jp in range(pairs_per_group):
            pair = g * pairs_per_group + jp
            ls = slice(pair * LANES, (pair + 1) * LANES)
            xs16 = xbs[:, ls].astype(BF16)
            ys = []
            for j in (2 * pair, 2 * pair + 1):
                diff = acum[:, j:j + 1] - acum_t[j:j + 1, :]
                mj = cb * jnp.exp(jnp.where(causal, diff, NEG_BIG)) * dt_t[j:j + 1, :]
                ys.append(_dot(mj.astype(BF16), xs16))
            y_intra = jnp.where(half, ys[0], ys[1])
            ybuf[:, ls] = y_intra + y_inter[:, jp * LANES:(jp + 1) * LANES] * ea_x[:, ls]
            if jp % 2 == 1:
                mxu_fill()

    for cbk in range(conv_ch // LANES):
        mxu_fill()
        cs = slice(cbk * LANES, (cbk + 1) * LANES)
        _causal_taps(apad, cs, CONV_HIST_ROWS, cw_ref, cb_ref, CONV_WIDTH, tl, rolls.at[cbk % 2], cout)
    conv = cout[...]
    mu = jnp.mean(conv, axis=-1, keepdims=True)
    xc = conv - mu
    cn = xc * lax.rsqrt(jnp.mean(xc * xc, axis=-1, keepdims=True) + EPS) * lng_ref[...] + lnb_ref[...]
    cact = cn * jax.nn.sigmoid(cn)
    for _ in mxu_todo:
        pass

    y = ybuf[0:tl, :] + dexp_ref[...] * xbs[0:tl, 0:inner]
    z = late["z"]
    y = _rms(y * (z * jax.nn.sigmoid(z)), sng_ref[...])

    o_ref[...] += _dot(cact.astype(BF16), wout_ref[0:conv_ch, :]) + _dot(y.astype(BF16), wout_ref[conv_ch:, :])

    apad[0:CONV_HIST_ROWS, :] = apad[tl:tl + CONV_HIST_ROWS, :]
    xpad[0:SSD_HIST_ROWS, :] = xpad[tl:tl + SSD_HIST_ROWS, :]

    if fuse_ffn1:
        h1s[s % 2] = x_ref[...] + 0.5 * facc[...]

    @pl.when(c == chunks_per_seq - 1)
    def _():
        ncb_ref[...] = apad[ch0:CONV_HIST_ROWS, :]
        nxb_ref[...] = xpad[xh0:SSD_HIST_ROWS, :]
        nss_ref[...] = st[...].T


def _ffn1(x, ng, wg, wu, wd):
    n, d = x.shape
    tm = _token_tile(n)
    row = pl.BlockSpec((tm, d), lambda i: (i, 0))
    weights = (ng, wg, wu, wd)
    return pl.pallas_call(
        _ffn1_body,
        grid=(n // tm,),
        in_specs=[row] + [_resident(w.shape) for w in weights],
        out_specs=row,
        out_shape=jax.ShapeDtypeStruct((n, d), F32),
        compiler_params=pltpu.CompilerParams(dimension_semantics=("arbitrary",), vmem_limit_bytes=VMEM_LIMIT_BYTES),
        name="ffn1",
    )(x, *weights)


def _ffn1_mixer(x, conv_buf, xbc_buf, ssm, f1_weights, mix_weights):
    b, t, d = x.shape
    tl = 256 if t % 256 == 0 else t
    assert tl % SUBLANES == 0 and tl >= CONV_HIST_ROWS
    lp = max(tl, LANES)
    cps = t // tl
    n = b * cps
    fuse_ffn1 = cps > 1
    x = x.reshape(b * t, d)
    w_specs = [_resident(w.shape) for w in mix_weights]
    if fuse_ffn1:
        w_specs = [_resident(w.shape) for w in f1_weights] + w_specs
        weights = tuple(f1_weights) + tuple(mix_weights)
        steps = n + 1
        mix_chunk = lambda s: jnp.maximum(s - 1, 0)
        x_chunk = lambda s: jnp.minimum(s, n - 1)
    else:
        x = _ffn1(x, *f1_weights)
        weights = tuple(mix_weights)
        steps = n
        mix_chunk = x_chunk = lambda s: s
    conv_ch = conv_buf.shape[-1]
    xbc_dim = xbc_buf.shape[-1]
    inner = SSD_HEADS * SSD_HEAD_DIM
    ssm2 = ssm.reshape(b, inner, SSD_STATE)
    seq = lambda rows, cols: pl.BlockSpec((None, rows, cols), lambda s: (mix_chunk(s) // cps, 0, 0))
    out = pl.pallas_call(
        functools.partial(_mixer_body, tl=tl, chunks_per_seq=cps, fuse_ffn1=fuse_ffn1),
        grid=(steps,),
        in_specs=[pl.BlockSpec((tl, d), lambda s: (x_chunk(s), 0)),
                  seq(CONV_WIDTH - 1, conv_ch), seq(SSD_CONV_WIDTH - 1, xbc_dim), seq(inner, SSD_STATE)] + w_specs,
        out_specs=[pl.BlockSpec((tl, d), lambda s: (mix_chunk(s), 0)),
                   seq(CONV_WIDTH - 1, conv_ch), seq(SSD_CONV_WIDTH - 1, xbc_dim), seq(inner, SSD_STATE)],
        out_shape=[jax.ShapeDtypeStruct((b * t, d), F32),
                   jax.ShapeDtypeStruct(conv_buf.shape, F32),
                   jax.ShapeDtypeStruct(xbc_buf.shape, F32),
                   jax.ShapeDtypeStruct(ssm2.shape, F32)],
        scratch_shapes=[pltpu.VMEM((2, tl, d), F32),
                        pltpu.VMEM((CONV_HIST_ROWS + tl, conv_ch), F32),
                        pltpu.VMEM((SSD_HIST_ROWS + tl, xbc_dim), F32),
                        pltpu.VMEM((SSD_STATE, inner), F32),
                        pltpu.VMEM((tl, d), BF16),
                        pltpu.VMEM((tl, d), BF16),
                        pltpu.VMEM((2, SUBLANES - 1, CONV_HIST_ROWS + tl, LANES), F32),
                        pltpu.VMEM((tl, conv_ch), F32),
                        pltpu.VMEM((lp, xbc_dim), F32),
                        pltpu.VMEM((tl, d), F32),
                        pltpu.VMEM((lp, inner), F32)],
        compiler_params=pltpu.CompilerParams(dimension_semantics=("arbitrary",), vmem_limit_bytes=VMEM_LIMIT_BYTES),
        name="ffn1_mixer" if fuse_ffn1 else "mixer",
    )(x, conv_buf, xbc_buf, ssm2, *weights)
    h2, ncb, nxb, nss = out
    return h2, ncb, nxb, nss.reshape(ssm.shape)


def _row(v):
    return v.reshape(1, -1).astype(F32)


def _pad_lanes(v, width):
    return jnp.pad(v, ((0, 0), (0, width - v.shape[1])))


def kernel(x_prompt, x_sample, p_prompt, p_sample, state_conv, state_ssd_conv, state_ssd,
           ffn1_norm, ffn1_w_gate, ffn1_w_up, ffn1_w_down, mix_norm, w_in,
           conv_dw_w, conv_dw_b, conv_ln_g, conv_ln_b, ssd_conv_w, ssd_conv_b,
           ssd_dt_bias, ssd_A_log, ssd_D, ssd_norm, w_out,
           ffn2_norm, ffn2_w_gate, ffn2_w_up, ffn2_w_down,
           ple_norm, ple_w_proj, ple_w_gate, final_norm):
    depth = ffn1_norm.shape[0]
    assert depth == 1, "the FFN2 kernel fuses the final norm, so it must be the last layer's"
    i = 0
    bp, sp, d = x_prompt.shape
    conv_ch = state_conv.shape[-1]
    xbc_dim = state_ssd_conv.shape[-1]
    head_expand = (jnp.arange(LANES)[:, None] == jnp.arange(SSD_HEADS * SSD_HEAD_DIM)[None, :] // SSD_HEAD_DIM).astype(BF16)
    head_expand = jnp.concatenate([head_expand, head_expand], axis=0)

    win = _pad_lanes(w_in[i].astype(BF16), w_in.shape[2] - SSD_HEADS + LANES)
    mix_w = (_row(mix_norm[i]), win, conv_dw_w[i], _row(conv_dw_b[i]), _row(conv_ln_g[i]), _row(conv_ln_b[i]),
             ssd_conv_w[i], _row(ssd_conv_b[i]), _pad_lanes(_row(ssd_dt_bias[i]), LANES),
             _pad_lanes(_row(ssd_A_log[i]), LANES), _row(jnp.repeat(ssd_D[i], SSD_HEAD_DIM)),
             _row(ssd_norm[i]), head_expand, w_out[i].astype(BF16))
    f1_w = (_row(ffn1_norm[i]), ffn1_w_gate[i].astype(BF16), ffn1_w_up[i].astype(BF16), ffn1_w_down[i].astype(BF16))
    f2_w = (_row(ffn2_norm[i]), ffn2_w_gate[i].astype(BF16), ffn2_w_up[i].astype(BF16), ffn2_w_down[i].astype(BF16),
            _row(ple_norm[i]), ple_w_proj[i].astype(BF16), ple_w_gate[i].astype(BF16), _row(final_norm))
    zero_hist = (jnp.zeros((bp, CONV_WIDTH - 1, conv_ch), F32), jnp.zeros((bp, SSD_CONV_WIDTH - 1, xbc_dim), F32),
                 jnp.zeros((bp,) + state_ssd.shape[2:], F32))
    outs = []
    for x, p, hist in ((x_prompt, p_prompt[i], zero_hist),
                       (x_sample, p_sample[i], (state_conv[i], state_ssd_conv[i], state_ssd[i]))):
        b, t, _ = x.shape
        h2, ncb, nxb, nss = _ffn1_mixer(x, *hist, f1_w, mix_w)
        y = _ffn2(h2, p.reshape(b * t, -1), *f2_w).reshape(b, t, d)
        outs.append((y, ncb[None], nxb[None], nss[None]))
    (y_p, c1, x1, s1), (y_s, c2, x2, s2) = outs
    return (y_p, y_s, c1, x1, s1, c2, x2, s2)
```

```python
import functools

import jax
import jax.numpy as jnp
from jax import lax
from jax.experimental import pallas as pl
from jax.experimental.pallas import tpu as pltpu

EPS = 1e-6
CONV_WIDTH = 31
SSD_CONV_WIDTH = 4
SSD_HEADS = 16
SSD_HEAD_DIM = 64
SSD_GROUPS = 2
SSD_STATE = 128
LANES = 128
SUBLANES = 8
MXU_DIM = 256
VMEM_LIMIT_BYTES = 56 * 1024 * 1024
CONV_HIST_ROWS = 32
SSD_HIST_ROWS = 8
NEG_BIG = -1e30
TAP_ROWS = 128

F32 = jnp.float32
BF16 = jnp.bfloat16


def _rms(x, g):
    ms = jnp.mean(x * x, axis=-1, keepdims=True)
    return x * lax.rsqrt(ms + EPS) * g


def _dot(a, b):
    return jnp.dot(a, b, preferred_element_type=F32)


def _split2(v):
    hi = v.astype(BF16)
    lo = (v - hi.astype(F32)).astype(BF16)
    return hi, lo


def _split3(v):
    hi = v.astype(BF16)
    r = v - hi.astype(F32)
    mid = r.astype(BF16)
    lo = (r - mid.astype(F32)).astype(BF16)
    return hi, mid, lo


def _causal_taps(ext_ref, cs, hist_rows, w_ref, b_ref, n_taps, tl, roll_ref, out_ref, act=None):
    rows = hist_rows + tl
    ext = ext_ref[:, cs]
    taps = []
    for k in range(n_taps):
        off = hist_rows - (n_taps - 1) + k
        r = (-off) % SUBLANES
        taps.append((k, r, off + r))
    rots = {0: ext.reshape(rows // SUBLANES, SUBLANES, LANES)}

    def tile_rotation(q):
        if q not in rots:
            rots[q] = pltpu.roll(tile_rotation(q - (q & -q)), q & -q, axis=1)
        return rots[q]

    sub = lax.broadcasted_iota(jnp.int32, (1, SUBLANES, LANES), 1)
    for r in sorted({r for _, r, _ in taps} - {0}):
        rot = tile_rotation(r)
        above = jnp.concatenate([rot[-1:], rot[:-1]], axis=0)
        roll_ref[r - 1, 0:rows, :] = jnp.where(sub >= r, rot, above).reshape(rows, LANES)
    for t0 in range(0, tl, TAP_ROWS):
        nr = min(TAP_ROWS, tl - t0)
        acc = jnp.broadcast_to(b_ref[:, cs], (nr, LANES))
        for k, r, start in taps:
            lo = start + t0
            win = ext_ref[lo:lo + nr, cs] if r == 0 else roll_ref[r - 1, lo:lo + nr, :]
            acc = acc + w_ref[k:k + 1, cs] * win
        out_ref[t0:t0 + nr, cs] = acc if act is None else act(acc)


def _ff_chunks(d_ff, n_chunks=4):
    tiles = d_ff // MXU_DIM
    assert tiles * MXU_DIM == d_ff and tiles >= n_chunks
    edges = [MXU_DIM * ((tiles * c + n_chunks - 1) // n_chunks) for c in range(n_chunks + 1)]
    return tuple(slice(lo, hi) for lo, hi in zip(edges[:-1], edges[1:]))


def _swiglu(xn, wg_ref, wu_ref, wd_ref):
    acc = None
    for cs in _ff_chunks(wg_ref.shape[1]):
        g = _dot(xn, wg_ref[:, cs])
        u = _dot(xn, wu_ref[:, cs])
        hm = (g * jax.nn.sigmoid(g) * u).astype(BF16)
        part = _dot(hm, wd_ref[cs, :])
        acc = part if acc is None else acc + part
    return acc


def _ffn1_body(x_ref, ng_ref, wg_ref, wu_ref, wd_ref, o_ref):
    x = x_ref[...]
    xn = _rms(x, ng_ref[...]).astype(BF16)
    o_ref[...] = x + 0.5 * _swiglu(xn, wg_ref, wu_ref, wd_ref)


def _ffn2_body(x_ref, p_ref, ng_ref, wg_ref, wu_ref, wd_ref, pn_ref, pwp_ref, pwg_ref, fn_ref, o_ref):
    x = x_ref[...]
    ple = _dot(p_ref[...].astype(BF16), pwp_ref[...])
    xn = _rms(x, ng_ref[...]).astype(BF16)
    h = x + 0.5 * _swiglu(xn, wg_ref, wu_ref, wd_ref)
    gate = jax.nn.sigmoid(_dot(_rms(h, pn_ref[...]).astype(BF16), pwg_ref[...]))
    h = h + gate * ple
    o_ref[...] = _rms(h, fn_ref[...])


def _resident(shape):
    nd = len(shape)
    return pl.BlockSpec(shape, lambda *_: (0,) * nd, pipeline_mode=pl.Buffered(1))


def _token_tile(n_tokens):
    for tm in (1024, 512, 256, 128, 64, 32, 16, 8):
        if n_tokens % tm == 0:
            return tm
    raise ValueError(f"token count {n_tokens} must be a multiple of {SUBLANES}")


def _ffn2(x, p, ng, wg, wu, wd, pn, pwp, pwg, fn):
    n, d = x.shape
    tm = _token_tile(n)
    row = pl.BlockSpec((tm, d), lambda i: (i, 0))
    prow = pl.BlockSpec((tm, p.shape[1]), lambda i: (i, 0))
    weights = (ng, wg, wu, wd, pn, pwp, pwg, fn)
    return pl.pallas_call(
        _ffn2_body,
        grid=(n // tm,),
        in_specs=[row, prow] + [_resident(w.shape) for w in weights],
        out_specs=row,
        out_shape=jax.ShapeDtypeStruct((n, d), F32),
        compiler_params=pltpu.CompilerParams(dimension_semantics=("arbitrary",), vmem_limit_bytes=VMEM_LIMIT_BYTES),
        name="ffn2",
    )(x, p, *weights)


def _mixer_body(*refs, tl, chunks_per_seq, fuse_ffn1):
    refs = list(refs)
    x_ref, cbuf_ref, xbuf_ref, ssm_ref = refs[:4]
    del refs[:4]
    if fuse_ffn1:
        f1g_ref, f1wg_ref, f1wu_ref, f1wd_ref = refs[:4]
        del refs[:4]
    (mixg_ref, win_ref, wdt_ref, cw_ref, cb_ref, lng_ref, lnb_ref, sw_ref, sb_ref, dtb_ref, alog_ref, dexp_ref,
     sng_ref, hexp_ref, wout_ref, o_ref, ncb_ref, nxb_ref, nss_ref,
     h1s, apad, xpad, st, hn_s, xn_s, rolls, cout, xbs, facc, ybuf) = refs

    s = pl.program_id(0)
    mix_chunk = jnp.maximum(s - 1, 0) if fuse_ffn1 else s
    c = mix_chunk % chunks_per_seq
    conv_ch = cw_ref.shape[1]
    xbc_dim = sw_ref.shape[1]
    inner = SSD_HEADS * SSD_HEAD_DIM
    gn = SSD_GROUPS * SSD_STATE
    gw = inner // SSD_GROUPS
    lp = max(tl, LANES)
    ch0 = CONV_HIST_ROWS - (CONV_WIDTH - 1)
    xh0 = SSD_HIST_ROWS - (SSD_CONV_WIDTH - 1)

    @pl.when(c == 0)
    def _():
        apad[0:ch0, :] = jnp.zeros((ch0, conv_ch), F32)
        apad[ch0:CONV_HIST_ROWS, :] = cbuf_ref[...]
        xpad[0:xh0, :] = jnp.zeros((xh0, xbc_dim), F32)
        xpad[xh0:SSD_HIST_ROWS, :] = xbuf_ref[...]
        st[...] = ssm_ref[...].T

    if fuse_ffn1:
        @pl.when(s == 0)
        def _():
            h1s[1] = jnp.zeros(h1s.shape[1:], F32)

        h = h1s[(s + 1) % 2]
    else:
        h = x_ref[...]
    o_ref[...] = h
    hn_s[...] = _rms(h, mixg_ref[...]).astype(BF16)
    o1, o2, o3 = conv_ch, 2 * conv_ch, 2 * conv_ch + inner
    o4 = o3 + xbc_dim

    late = {}
    if fuse_ffn1:
        xn_s[...] = _rms(x_ref[...], f1g_ref[...]).astype(BF16)

    def independent_dots():
        if fuse_ffn1:
            for i, cs in enumerate(_ff_chunks(f1wg_ref.shape[1])):
                g = _dot(xn_s[...], f1wg_ref[:, cs])
                yield
                hm = (g * jax.nn.sigmoid(g) * _dot(xn_s[...], f1wu_ref[:, cs])).astype(BF16)
                yield
                part = _dot(hm, f1wd_ref[cs, :])
                facc[...] = part if i == 0 else facc[...] + part
                yield
        late["z"] = _dot(hn_s[...],win_ref[:, o2:o3])
        yield

    mxu_todo = independent_dots()

    def mxu_fill():
        next(mxu_todo, None)

    xpad[SSD_HIST_ROWS:, :] = _dot(hn_s[...],win_ref[:, o3:o4])
    dtr = _dot(hn_s[...], wdt_ref[...]) + dtb_ref[...]
    for wb in range(conv_ch // MXU_DIM):
        ws = slice(wb * MXU_DIM, (wb + 1) * MXU_DIM)
        gate = _dot(hn_s[...],win_ref[:, o1 + wb * MXU_DIM:o1 + (wb + 1) * MXU_DIM])
        apad[CONV_HIST_ROWS:, ws] = _dot(hn_s[...],win_ref[:, ws]) * jax.nn.sigmoid(gate)

    for cbk in range(xbc_dim // LANES):
        cs = slice(cbk * LANES, (cbk + 1) * LANES)
        _causal_taps(xpad, cs, SSD_HIST_ROWS, sw_ref, sb_ref, SSD_CONV_WIDTH, tl, rolls.at[cbk % 2], xbs,
                     act=lambda v: v * jax.nn.sigmoid(v))
    if lp > tl:
        xbs[tl:lp, :] = jnp.zeros((lp - tl, xbc_dim), F32)
    mxu_fill()

    lane = lax.broadcasted_iota(jnp.int32, (1, LANES), 1)
    head_lane = lane < SSD_HEADS
    dt =jnp.maximum(dtr, 0.0) + jnp.log1p(jnp.exp(-jnp.abs(dtr)))
    dt = jnp.where(head_lane, dt, 0.0)
    a_neg = jnp.where(head_lane, -jnp.exp(alog_ref[...]), 0.0)
    if lp > tl:
        dt = jnp.concatenate([dt, jnp.zeros((lp - tl, LANES), F32)], axis=0)

    row_i = lax.broadcasted_iota(jnp.int32, (lp, lp), 0)
    col_i = lax.broadcasted_iota(jnp.int32, (lp, lp), 1)
    causal = row_i >= col_i
    tril = jnp.where(causal, 1.0, 0.0).astype(BF16)

    csum = _dot(tril, jnp.concatenate(_split3(dt * a_neg), axis=1))
    acum = csum[:, 0:LANES] + csum[:, LANES:2 * LANES] + csum[:, 2 * LANES:]
    alast = acum[lp - 1:lp, :]
    acum_t = acum.T
    dt_t = dt.T
    ea = jnp.exp(acum)
    wld = jnp.exp(alast - acum) * dt
    eal = jnp.broadcast_to(jnp.exp(alast), (SUBLANES, LANES))

    def expand(v):
        hi, lo = _split2(v)
        return _dot(jnp.concatenate([hi, lo], axis=1), hexp_ref[...])

    ea_x = expand(ea)
    wld_x = expand(wld)
    eal_x = expand(eal)[0:1, :]
    mxu_fill()

    half = lane < SSD_HEAD_DIM
    pairs_per_group = gw // LANES
    for g in range(SSD_GROUPS):
        gs = slice(g * gw, (g + 1) * gw)
        bs = slice(inner + g * SSD_STATE, inner + (g + 1) * SSD_STATE)
        b_t16 = xbs[:, bs].T.astype(BF16)
        c16 = xbs[:, gn + bs.start:gn + bs.stop].astype(BF16)
        cb = _dot(c16, b_t16)
        st_g = st[:, gs]
        y_inter = _dot(c16, st_g.astype(BF16))
        st[:, gs] = st_g * eal_x[:, gs] + _dot(b_t16, (wld_x[:, gs] * xbs[:, gs]).astype(BF16))
        for jp in range(pairs_per_group):
            pair = g * pairs_per_group + jp
            ls = slice(pair * LANES, (pair + 1) * LANES)
            xs16 = xbs[:, ls].astype(BF16)
            ys = []
            for j in (2 * pair, 2 * pair + 1):
                diff = acum[:, j:j + 1] - acum_t[j:j + 1, :]
                mj = cb * jnp.exp(jnp.where(causal, diff, NEG_BIG)) * dt_t[j:j + 1, :]
                ys.append(_dot(mj.astype(BF16), xs16))
            y_intra = jnp.where(half, ys[0], ys[1])
            ybuf[:, ls] = y_intra + y_inter[:, jp * LANES:(jp + 1) * LANES] * ea_x[:, ls]
            if jp % 2 == 1:
                mxu_fill()

    for cbk in range(conv_ch // LANES):
        mxu_fill()
        cs = slice(cbk * LANES, (cbk + 1) * LANES)
        _causal_taps(apad, cs, CONV_HIST_ROWS, cw_ref, cb_ref, CONV_WIDTH, tl, rolls.at[cbk % 2], cout)
    conv = cout[...]
    mu = jnp.mean(conv, axis=-1, keepdims=True)
    xc = conv - mu
    cn = xc * lax.rsqrt(jnp.mean(xc * xc, axis=-1, keepdims=True) + EPS) * lng_ref[...] + lnb_ref[...]
    cact = cn * jax.nn.sigmoid(cn)
    for _ in mxu_todo:
        pass

    y = ybuf[0:tl, :] + dexp_ref[...] * xbs[0:tl, 0:inner]
    z = late["z"]
    y = _rms(y * (z * jax.nn.sigmoid(z)), sng_ref[...])

    o_ref[...] += _dot(cact.astype(BF16), wout_ref[0:conv_ch, :]) + _dot(y.astype(BF16), wout_ref[conv_ch:, :])

    apad[0:CONV_HIST_ROWS, :] = apad[tl:tl + CONV_HIST_ROWS, :]
    xpad[0:SSD_HIST_ROWS, :] = xpad[tl:tl + SSD_HIST_ROWS, :]

    if fuse_ffn1:
        h1s[s % 2] = x_ref[...] + 0.5 * facc[...]

    @pl.when(c == chunks_per_seq - 1)
    def _():
        ncb_ref[...] = apad[ch0:CONV_HIST_ROWS, :]
        nxb_ref[...] = xpad[xh0:SSD_HIST_ROWS, :]
        nss_ref[...] = st[...].T


def _ffn1(x, ng, wg, wu, wd):
    n, d = x.shape
    tm = _token_tile(n)
    row = pl.BlockSpec((tm, d), lambda i: (i, 0))
    weights = (ng, wg, wu, wd)
    return pl.pallas_call(
        _ffn1_body,
        grid=(n // tm,),
        in_specs=[row] + [_resident(w.shape) for w in weights],
        out_specs=row,
        out_shape=jax.ShapeDtypeStruct((n, d), F32),
        compiler_params=pltpu.CompilerParams(dimension_semantics=("arbitrary",), vmem_limit_bytes=VMEM_LIMIT_BYTES),
        name="ffn1",
    )(x, *weights)


def _ffn1_mixer(x, conv_buf, xbc_buf, ssm, f1_weights, mix_weights):
    b, t, d = x.shape
    tl = 256 if t % 256 == 0 else t
    assert tl % SUBLANES == 0 and tl >= CONV_HIST_ROWS
    lp = max(tl, LANES)
    cps = t // tl
    n = b * cps
    fuse_ffn1 = cps > 1
    x = x.reshape(b * t, d)
    w_specs = [_resident(w.shape) for w in mix_weights]
    win = mix_weights[1]
    w_specs[1] = _resident((win.shape[0], win.shape[1] // LANES * LANES))
    if fuse_ffn1:
        w_specs = [_resident(w.shape) for w in f1_weights] + w_specs
        weights = tuple(f1_weights) + tuple(mix_weights)
        steps = n + 1
        mix_chunk = lambda s: jnp.maximum(s - 1, 0)
        x_chunk = lambda s: jnp.minimum(s, n - 1)
    else:
        x = _ffn1(x, *f1_weights)
        weights = tuple(mix_weights)
        steps = n
        mix_chunk = x_chunk = lambda s: s
    conv_ch = conv_buf.shape[-1]
    xbc_dim = xbc_buf.shape[-1]
    inner = SSD_HEADS * SSD_HEAD_DIM
    ssm2 = ssm.reshape(b, inner, SSD_STATE)
    seq = lambda rows, cols: pl.BlockSpec((None, rows, cols), lambda s: (mix_chunk(s) // cps, 0, 0))
    out = pl.pallas_call(
        functools.partial(_mixer_body, tl=tl, chunks_per_seq=cps, fuse_ffn1=fuse_ffn1),
        grid=(steps,),
        in_specs=[pl.BlockSpec((tl, d), lambda s: (x_chunk(s), 0)),
                  seq(CONV_WIDTH - 1, conv_ch), seq(SSD_CONV_WIDTH - 1, xbc_dim), seq(inner, SSD_STATE)] + w_specs,
        out_specs=[pl.BlockSpec((tl, d), lambda s: (mix_chunk(s), 0)),
                   seq(CONV_WIDTH - 1, conv_ch), seq(SSD_CONV_WIDTH - 1, xbc_dim), seq(inner, SSD_STATE)],
        out_shape=[jax.ShapeDtypeStruct((b * t, d), F32),
                   jax.ShapeDtypeStruct(conv_buf.shape, F32),
                   jax.ShapeDtypeStruct(xbc_buf.shape, F32),
                   jax.ShapeDtypeStruct(ssm2.shape, F32)],
        scratch_shapes=[pltpu.VMEM((2, tl, d), F32),
                        pltpu.VMEM((CONV_HIST_ROWS + tl, conv_ch), F32),
                        pltpu.VMEM((SSD_HIST_ROWS + tl, xbc_dim), F32),
                        pltpu.VMEM((SSD_STATE, inner), F32),
                        pltpu.VMEM((tl, d), BF16),
                        pltpu.VMEM((tl, d), BF16),
                        pltpu.VMEM((2, SUBLANES - 1, CONV_HIST_ROWS + tl, LANES), F32),
                        pltpu.VMEM((tl, conv_ch), F32),
                        pltpu.VMEM((lp, xbc_dim), F32),
                        pltpu.VMEM((tl, d), F32),
                        pltpu.VMEM((lp, inner), F32)],
        compiler_params=pltpu.CompilerParams(dimension_semantics=("arbitrary",), vmem_limit_bytes=VMEM_LIMIT_BYTES),
        name="ffn1_mixer" if fuse_ffn1 else "mixer",
    )(x, conv_buf, xbc_buf, ssm2, *weights)
    h2, ncb, nxb, nss = out
    return h2, ncb, nxb, nss.reshape(ssm.shape)


def _row(v):
    return v.reshape(1, -1).astype(F32)


def _pad_lanes(v, width):
    return jnp.pad(v, ((0, 0), (0, width - v.shape[1])))


def kernel(x_prompt, x_sample, p_prompt, p_sample, state_conv, state_ssd_conv, state_ssd,
           ffn1_norm, ffn1_w_gate, ffn1_w_up, ffn1_w_down, mix_norm, w_in,
           conv_dw_w, conv_dw_b, conv_ln_g, conv_ln_b, ssd_conv_w, ssd_conv_b,
           ssd_dt_bias, ssd_A_log, ssd_D, ssd_norm, w_out,
           ffn2_norm, ffn2_w_gate, ffn2_w_up, ffn2_w_down,
           ple_norm, ple_w_proj, ple_w_gate, final_norm):
    depth = ffn1_norm.shape[0]
    assert depth == 1, "the FFN2 kernel fuses the final norm, so it must be the last layer's"
    i = 0
    bp, sp, d = x_prompt.shape
    conv_ch = state_conv.shape[-1]
    xbc_dim = state_ssd_conv.shape[-1]
    head_expand = (jnp.arange(LANES)[:, None] == jnp.arange(SSD_HEADS * SSD_HEAD_DIM)[None, :] // SSD_HEAD_DIM).astype(BF16)
    head_expand = jnp.concatenate([head_expand, head_expand], axis=0)

    n_dt = w_in.shape[2] - SSD_HEADS
    win = w_in[i].astype(BF16)
    wdt = _pad_lanes(w_in[i][:, n_dt:], LANES).astype(BF16)
    mix_w = (_row(mix_norm[i]), win, wdt, conv_dw_w[i], _row(conv_dw_b[i]), _row(conv_ln_g[i]), _row(conv_ln_b[i]),
             ssd_conv_w[i], _row(ssd_conv_b[i]), _pad_lanes(_row(ssd_dt_bias[i]), LANES),
             _pad_lanes(_row(ssd_A_log[i]), LANES), _row(jnp.repeat(ssd_D[i], SSD_HEAD_DIM)),
             _row(ssd_norm[i]), head_expand, w_out[i].astype(BF16))
    f1_w = (_row(ffn1_norm[i]), ffn1_w_gate[i].astype(BF16), ffn1_w_up[i].astype(BF16), ffn1_w_down[i].astype(BF16))
    f2_w = (_row(ffn2_norm[i]), ffn2_w_gate[i].astype(BF16), ffn2_w_up[i].astype(BF16), ffn2_w_down[i].astype(BF16),
            _row(ple_norm[i]), ple_w_proj[i].astype(BF16), ple_w_gate[i].astype(BF16), _row(final_norm))
    zero_hist = (jnp.zeros((bp, CONV_WIDTH - 1, conv_ch), F32), jnp.zeros((bp, SSD_CONV_WIDTH - 1, xbc_dim), F32),
                 jnp.zeros((bp,) + state_ssd.shape[2:], F32))
    outs = []
    for x, p, hist in ((x_prompt, p_prompt[i], zero_hist),
                       (x_sample, p_sample[i], (state_conv[i], state_ssd_conv[i], state_ssd[i]))):
        b, t, _ = x.shape
        h2, ncb, nxb, nss = _ffn1_mixer(x, *hist, f1_w, mix_w)
        y = _ffn2(h2, p.reshape(b * t, -1), *f2_w).reshape(b, t, d)
        outs.append((y, ncb[None], nxb[None], nss[None]))
    (y_p, c1, x1, s1), (y_s, c2, x2, s2) = outs
    return (y_p, y_s, c1, x1, s1, c2, x2, s2)
```

```python
import functools

import jax
import jax.numpy as jnp
from jax import lax
from jax.experimental import pallas as pl
from jax.experimental.pallas import tpu as pltpu

EPS = 1e-6
CONV_WIDTH = 31
SSD_CONV_WIDTH = 4
SSD_HEADS = 16
SSD_HEAD_DIM = 64
SSD_GROUPS = 2
SSD_STATE = 128
LANES = 128
SUBLANES = 8
MXU_DIM = 256
VMEM_LIMIT_BYTES = 56 * 1024 * 1024
CONV_HIST_ROWS = 32
SSD_HIST_ROWS = 8
NEG_BIG = -1e30
TAP_ROWS = 128

F32 = jnp.float32
BF16 = jnp.bfloat16


def _rms(x, g):
    ms = jnp.mean(x * x, axis=-1, keepdims=True)
    return x * lax.rsqrt(ms + EPS) * g


def _dot(a, b):
    return jnp.dot(a, b, preferred_element_type=F32)


def _split2(v):
    hi = v.astype(BF16)
    lo = (v - hi.astype(F32)).astype(BF16)
    return hi, lo


def _split3(v):
    hi = v.astype(BF16)
    r = v - hi.astype(F32)
    mid = r.astype(BF16)
    lo = (r - mid.astype(F32)).astype(BF16)
    return hi, mid, lo


def _causal_taps(ext_ref, cs, hist_rows, w_ref, b_ref, n_taps, tl, roll_ref, out_ref, act=None):
    rows = hist_rows + tl
    ext = ext_ref[:, cs]
    taps = []
    for k in range(n_taps):
        off = hist_rows - (n_taps - 1) + k
        r = (-off) % SUBLANES
        taps.append((k, r, off + r))
    rots = {0: ext.reshape(rows // SUBLANES, SUBLANES, LANES)}

    def tile_rotation(q):
        if q not in rots:
            rots[q] = pltpu.roll(tile_rotation(q - (q & -q)), q & -q, axis=1)
        return rots[q]

    sub = lax.broadcasted_iota(jnp.int32, (1, SUBLANES, LANES), 1)
    for r in sorted({r for _, r, _ in taps} - {0}):
        rot = tile_rotation(r)
        above = jnp.concatenate([rot[-1:], rot[:-1]], axis=0)
        roll_ref[r - 1, 0:rows, :] = jnp.where(sub >= r, rot, above).reshape(rows, LANES)
    for t0 in range(0, tl, TAP_ROWS):
        nr = min(TAP_ROWS, tl - t0)
        acc = jnp.broadcast_to(b_ref[:, cs], (nr, LANES))
        for k, r, start in taps:
            lo = start + t0
            win = ext_ref[lo:lo + nr, cs] if r == 0 else roll_ref[r - 1, lo:lo + nr, :]
            acc = acc + w_ref[k:k + 1, cs] * win
        out_ref[t0:t0 + nr, cs] = acc if act is None else act(acc)


def _ff_chunks(d_ff, n_chunks=4):
    tiles = d_ff // MXU_DIM
    assert tiles * MXU_DIM == d_ff and tiles >= n_chunks
    edges = [MXU_DIM * ((tiles * c + n_chunks - 1) // n_chunks) for c in range(n_chunks + 1)]
    return tuple(slice(lo, hi) for lo, hi in zip(edges[:-1], edges[1:]))


def _swiglu(xn, wg_ref, wu_ref, wd_ref):
    acc = None
    for cs in _ff_chunks(wg_ref.shape[1]):
        g = _dot(xn, wg_ref[:, cs])
        u = _dot(xn, wu_ref[:, cs])
        hm = (g * jax.nn.sigmoid(g) * u).astype(BF16)
        part = _dot(hm, wd_ref[cs, :])
        acc = part if acc is None else acc + part
    return acc


def _ffn1_body(x_ref, ng_ref, wg_ref, wu_ref, wd_ref, o_ref):
    x = x_ref[...]
    xn = _rms(x, ng_ref[...]).astype(BF16)
    o_ref[...] = x + 0.5 * _swiglu(xn, wg_ref, wu_ref, wd_ref)


def _ffn2_body(x_ref, p_ref, ng_ref, wg_ref, wu_ref, wd_ref, pn_ref, pwp_ref, pwg_ref, fn_ref, o_ref):
    x = x_ref[...]
    ple = _dot(p_ref[...].astype(BF16), pwp_ref[...])
    xn = _rms(x, ng_ref[...]).astype(BF16)
    h = x + 0.5 * _swiglu(xn, wg_ref, wu_ref, wd_ref)
    gate = jax.nn.sigmoid(_dot(_rms(h, pn_ref[...]).astype(BF16), pwg_ref[...]))
    h = h + gate * ple
    o_ref[...] = _rms(h, fn_ref[...])


def _resident(shape):
    nd = len(shape)
    return pl.BlockSpec(shape, lambda *_: (0,) * nd, pipeline_mode=pl.Buffered(1))


def _token_tile(n_tokens):
    for tm in (1024, 512, 256, 128, 64, 32, 16, 8):
        if n_tokens % tm == 0:
            return tm
    raise ValueError(f"token count {n_tokens} must be a multiple of {SUBLANES}")


def _ffn2(x, p, ng, wg, wu, wd, pn, pwp, pwg, fn):
    n, d = x.shape
    tm = _token_tile(n)
    row = pl.BlockSpec((tm, d), lambda i: (i, 0))
    prow = pl.BlockSpec((tm, p.shape[1]), lambda i: (i, 0))
    weights = (ng, wg, wu, wd, pn, pwp, pwg, fn)
    return pl.pallas_call(
        _ffn2_body,
        grid=(n // tm,),
        in_specs=[row, prow] + [_resident(w.shape) for w in weights],
        out_specs=row,
        out_shape=jax.ShapeDtypeStruct((n, d), F32),
        compiler_params=pltpu.CompilerParams(dimension_semantics=("arbitrary",), vmem_limit_bytes=VMEM_LIMIT_BYTES),
        name="ffn2",
    )(x, p, *weights)


def _mixer_body(*refs, tl, chunks_per_seq, fuse_ffn1):
    refs = list(refs)
    x_ref, cbuf_ref, xbuf_ref, ssm_ref = refs[:4]
    del refs[:4]
    if fuse_ffn1:
        f1g_ref, f1wg_ref, f1wu_ref, f1wd_ref = refs[:4]
        del refs[:4]
    (mixg_ref, win_ref, wdt_ref, cw_ref, cb_ref, lng_ref, lnb_ref, sw_ref, sb_ref, dtb_ref, alog_ref, dexp_ref,
     sng_ref, hexp_ref, wout_ref, o_ref, ncb_ref, nxb_ref, nss_ref,
     h1s, apad, xpad, st, hn_s, xn_s, rolls, cout, xbs, facc, ybuf) = refs

    s = pl.program_id(0)
    mix_chunk = jnp.maximum(s - 1, 0) if fuse_ffn1 else s
    c = mix_chunk % chunks_per_seq
    conv_ch = cw_ref.shape[1]
    xbc_dim = sw_ref.shape[1]
    inner = SSD_HEADS * SSD_HEAD_DIM
    gn = SSD_GROUPS * SSD_STATE
    gw = inner // SSD_GROUPS
    lp = max(tl, LANES)
    ch0 = CONV_HIST_ROWS - (CONV_WIDTH - 1)
    xh0 = SSD_HIST_ROWS - (SSD_CONV_WIDTH - 1)

    @pl.when(c == 0)
    def _():
        apad[0:ch0, :] = jnp.zeros((ch0, conv_ch), F32)
        apad[ch0:CONV_HIST_ROWS, :] = cbuf_ref[...]
        xpad[0:xh0, :] = jnp.zeros((xh0, xbc_dim), F32)
        xpad[xh0:SSD_HIST_ROWS, :] = xbuf_ref[...]
        st[...] = ssm_ref[...].T

    if fuse_ffn1:
        @pl.when(s == 0)
        def _():
            h1s[1] = jnp.zeros(h1s.shape[1:], F32)

        h = h1s[(s + 1) % 2]
    else:
        h = x_ref[...]
    o_ref[...] = h
    hn_s[...] = _rms(h, mixg_ref[...]).astype(BF16)
    o1, o2, o3 = conv_ch, 2 * conv_ch, 2 * conv_ch + inner
    o4 = o3 + xbc_dim

    late = {}
    if fuse_ffn1:
        xn_s[...] = _rms(x_ref[...], f1g_ref[...]).astype(BF16)

    def independent_dots():
        if fuse_ffn1:
            for i, cs in enumerate(_ff_chunks(f1wg_ref.shape[1])):
                g = _dot(xn_s[...], f1wg_ref[:, cs])
                yield
                hm = (g * jax.nn.sigmoid(g) * _dot(xn_s[...], f1wu_ref[:, cs])).astype(BF16)
                yield
                part = _dot(hm, f1wd_ref[cs, :])
                facc[...] = part if i == 0 else facc[...] + part
                yield
        late["z"] = _dot(hn_s[...],win_ref[:, o2:o3])
        yield

    mxu_todo = independent_dots()

    def mxu_fill():
        next(mxu_todo, None)

    xpad[SSD_HIST_ROWS:, :] = _dot(hn_s[...],win_ref[:, o3:o4])
    dtr = _dot(hn_s[...], wdt_ref[...]) + dtb_ref[...]
    for wb in range(conv_ch // MXU_DIM):
        ws = slice(wb * MXU_DIM, (wb + 1) * MXU_DIM)
        gate = _dot(hn_s[...],win_ref[:, o1 + wb * MXU_DIM:o1 + (wb + 1) * MXU_DIM])
        apad[CONV_HIST_ROWS:, ws] = _dot(hn_s[...],win_ref[:, ws]) * jax.nn.sigmoid(gate)

    for cbk in range(xbc_dim // LANES):
        cs = slice(cbk * LANES, (cbk + 1) * LANES)
        _causal_taps(xpad, cs, SSD_HIST_ROWS, sw_ref, sb_ref, SSD_CONV_WIDTH, tl, rolls.at[cbk % 2], xbs,
                     act=lambda v: v * jax.nn.sigmoid(v))
    if lp > tl:
        xbs[tl:lp, :] = jnp.zeros((lp - tl, xbc_dim), F32)
    mxu_fill()

    lane = lax.broadcasted_iota(jnp.int32, (1, LANES), 1)
    head_lane = lane < SSD_HEADS
    dt =jnp.maximum(dtr, 0.0) + jnp.log1p(jnp.exp(-jnp.abs(dtr)))
    dt = jnp.where(head_lane, dt, 0.0)
    a_neg = jnp.where(head_lane, -jnp.exp(alog_ref[...]), 0.0)
    if lp > tl:
        dt = jnp.concatenate([dt, jnp.zeros((lp - tl, LANES), F32)], axis=0)

    row_i = lax.broadcasted_iota(jnp.int32, (lp, lp), 0)
    col_i = lax.broadcasted_iota(jnp.int32, (lp, lp), 1)
    causal = row_i >= col_i
    tril = jnp.where(causal, 1.0, 0.0).astype(BF16)

    csum = _dot(tril, jnp.concatenate(_split3(dt * a_neg), axis=1))
    acum = csum[:, 0:LANES] + csum[:, LANES:2 * LANES] + csum[:, 2 * LANES:]
    alast = acum[lp - 1:lp, :]
    acum_t = acum.T
    dt_t = dt.T
    ea = jnp.exp(acum)
    wld = jnp.exp(alast - acum) * dt
    eal = jnp.broadcast_to(jnp.exp(alast), (SUBLANES, LANES))

    def expand(v):
        hi, lo = _split2(v)
        return _dot(jnp.concatenate([hi, lo], axis=1), hexp_ref[...])

    ea_x = expand(ea)
    wld_x = expand(wld)
    eal_x = expand(eal)[0:1, :]
    mxu_fill()

    half = lane < SSD_HEAD_DIM
    pairs_per_group = gw // LANES
    for g in range(SSD_GROUPS):
        gs = slice(g * gw, (g + 1) * gw)
        bs = slice(inner + g * SSD_STATE, inner + (g + 1) * SSD_STATE)
        b_t16 = xbs[:, bs].T.astype(BF16)
        c16 = xbs[:, gn + bs.start:gn + bs.stop].astype(BF16)
        cb = _dot(c16, b_t16)
        st_g = st[:, gs]
        y_inter = _dot(c16, st_g.astype(BF16))
        st[:, gs] = st_g * eal_x[:, gs] + _dot(b_t16, (wld_x[:, gs] * xbs[:, gs]).astype(BF16))
        for jp in range(pairs_per_group):
            pair = g * pairs_per_group + jp
            ls = slice(pair * LANES, (pair + 1) * LANES)
            xs16 = xbs[:, ls].astype(BF16)
            ys = []
            for j in (2 * pair, 2 * pair + 1):
                diff = acum[:, j:j + 1] - acum_t[j:j + 1, :]
                mj = cb * jnp.exp(jnp.where(causal, diff, NEG_BIG)) * dt_t[j:j + 1, :]
                ys.append(_dot(mj.astype(BF16), xs16))
            y_intra = jnp.where(half, ys[0], ys[1])
            ybuf[:, ls] = y_intra + y_inter[:, jp * LANES:(jp + 1) * LANES] * ea_x[:, ls]

    for cbk in range(conv_ch // LANES):
        mxu_fill()
        if cbk % 2 == 1:
            mxu_fill()
        cs = slice(cbk * LANES, (cbk + 1) * LANES)
        _causal_taps(apad, cs, CONV_HIST_ROWS, cw_ref, cb_ref, CONV_WIDTH, tl, rolls.at[cbk % 2], cout)
    conv = cout[...]
    mu = jnp.mean(conv, axis=-1, keepdims=True)
    xc = conv - mu
    cn = xc * lax.rsqrt(jnp.mean(xc * xc, axis=-1, keepdims=True) + EPS) * lng_ref[...] + lnb_ref[...]
    cact = cn * jax.nn.sigmoid(cn)
    for _ in mxu_todo:
        pass

    y = ybuf[0:tl, :] + dexp_ref[...] * xbs[0:tl, 0:inner]
    z = late["z"]
    y = _rms(y * (z * jax.nn.sigmoid(z)), sng_ref[...])

    o_ref[...] += _dot(cact.astype(BF16), wout_ref[0:conv_ch, :]) + _dot(y.astype(BF16), wout_ref[conv_ch:, :])

    apad[0:CONV_HIST_ROWS, :] = apad[tl:tl + CONV_HIST_ROWS, :]
    xpad[0:SSD_HIST_ROWS, :] = xpad[tl:tl + SSD_HIST_ROWS, :]

    if fuse_ffn1:
        h1s[s % 2] = x_ref[...] + 0.5 * facc[...]

    @pl.when(c == chunks_per_seq - 1)
    def _():
        ncb_ref[...] = apad[ch0:CONV_HIST_ROWS, :]
        nxb_ref[...] = xpad[xh0:SSD_HIST_ROWS, :]
        nss_ref[...] = st[...].T


def _ffn1(x, ng, wg, wu, wd):
    n, d = x.shape
    tm = _token_tile(n)
    row = pl.BlockSpec((tm, d), lambda i: (i, 0))
    weights = (ng, wg, wu, wd)
    return pl.pallas_call(
        _ffn1_body,
        grid=(n // tm,),
        in_specs=[row] + [_resident(w.shape) for w in weights],
        out_specs=row,
        out_shape=jax.ShapeDtypeStruct((n, d), F32),
        compiler_params=pltpu.CompilerParams(dimension_semantics=("arbitrary",), vmem_limit_bytes=VMEM_LIMIT_BYTES),
        name="ffn1",
    )(x, *weights)


def _ffn1_mixer(x, conv_buf, xbc_buf, ssm, f1_weights, mix_weights):
    b, t, d = x.shape
    tl = 256 if t % 256 == 0 else t
    assert tl % SUBLANES == 0 and tl >= CONV_HIST_ROWS
    lp = max(tl, LANES)
    cps = t // tl
    n = b * cps
    fuse_ffn1 = cps > 1
    x = x.reshape(b * t, d)
    w_specs = [_resident(w.shape) for w in mix_weights]
    win = mix_weights[1]
    w_specs[1] = _resident((win.shape[0], win.shape[1] // LANES * LANES))
    if fuse_ffn1:
        w_specs = [_resident(w.shape) for w in f1_weights] + w_specs
        weights = tuple(f1_weights) + tuple(mix_weights)
        steps = n + 1
        mix_chunk = lambda s: jnp.maximum(s - 1, 0)
        x_chunk = lambda s: jnp.minimum(s, n - 1)
    else:
        x = _ffn1(x, *f1_weights)
        weights = tuple(mix_weights)
        steps = n
        mix_chunk = x_chunk = lambda s: s
    conv_ch = conv_buf.shape[-1]
    xbc_dim = xbc_buf.shape[-1]
    inner = SSD_HEADS * SSD_HEAD_DIM
    ssm2 = ssm.reshape(b, inner, SSD_STATE)
    seq = lambda rows, cols: pl.BlockSpec((None, rows, cols), lambda s: (mix_chunk(s) // cps, 0, 0))
    out = pl.pallas_call(
        functools.partial(_mixer_body, tl=tl, chunks_per_seq=cps, fuse_ffn1=fuse_ffn1),
        grid=(steps,),
        in_specs=[pl.BlockSpec((tl, d), lambda s: (x_chunk(s), 0)),
                  seq(CONV_WIDTH - 1, conv_ch), seq(SSD_CONV_WIDTH - 1, xbc_dim), seq(inner, SSD_STATE)] + w_specs,
        out_specs=[pl.BlockSpec((tl, d), lambda s: (mix_chunk(s), 0)),
                   seq(CONV_WIDTH - 1, conv_ch), seq(SSD_CONV_WIDTH - 1, xbc_dim), seq(inner, SSD_STATE)],
        out_shape=[jax.ShapeDtypeStruct((b * t, d), F32),
                   jax.ShapeDtypeStruct(conv_buf.shape, F32),
                   jax.ShapeDtypeStruct(xbc_buf.shape, F32),
                   jax.ShapeDtypeStruct(ssm2.shape, F32)],
        scratch_shapes=[pltpu.VMEM((2, tl, d), F32),
                        pltpu.VMEM((CONV_HIST_ROWS + tl, conv_ch), F32),
                        pltpu.VMEM((SSD_HIST_ROWS + tl, xbc_dim), F32),
                        pltpu.VMEM((SSD_STATE, inner), F32),
                        pltpu.VMEM((tl, d), BF16),
                        pltpu.VMEM((tl, d), BF16),
                        pltpu.VMEM((2, SUBLANES - 1, CONV_HIST_ROWS + tl, LANES), F32),
                        pltpu.VMEM((tl, conv_ch), F32),
                        pltpu.VMEM((lp, xbc_dim), F32),
                        pltpu.VMEM((tl, d), F32),
                        pltpu.VMEM((lp, inner), F32)],
        compiler_params=pltpu.CompilerParams(dimension_semantics=("arbitrary",), vmem_limit_bytes=VMEM_LIMIT_BYTES),
        name="ffn1_mixer" if fuse_ffn1 else "mixer",
    )(x, conv_buf, xbc_buf, ssm2, *weights)
    h2, ncb, nxb, nss = out
    return h2, ncb, nxb, nss.reshape(ssm.shape)


def _row(v):
    return v.reshape(1, -1).astype(F32)


def _pad_lanes(v, width):
    return jnp.pad(v, ((0, 0), (0, width - v.shape[1])))


def kernel(x_prompt, x_sample, p_prompt, p_sample, state_conv, state_ssd_conv, state_ssd,
           ffn1_norm, ffn1_w_gate, ffn1_w_up, ffn1_w_down, mix_norm, w_in,
           conv_dw_w, conv_dw_b, conv_ln_g, conv_ln_b, ssd_conv_w, ssd_conv_b,
           ssd_dt_bias, ssd_A_log, ssd_D, ssd_norm, w_out,
           ffn2_norm, ffn2_w_gate, ffn2_w_up, ffn2_w_down,
           ple_norm, ple_w_proj, ple_w_gate, final_norm):
    depth = ffn1_norm.shape[0]
    assert depth == 1, "the FFN2 kernel fuses the final norm, so it must be the last layer's"
    i = 0
    bp, sp, d = x_prompt.shape
    conv_ch = state_conv.shape[-1]
    xbc_dim = state_ssd_conv.shape[-1]
    head_expand = (jnp.arange(LANES)[:, None] == jnp.arange(SSD_HEADS * SSD_HEAD_DIM)[None, :] // SSD_HEAD_DIM).astype(BF16)
    head_expand = jnp.concatenate([head_expand, head_expand], axis=0)

    n_dt = w_in.shape[2] - SSD_HEADS
    win = w_in[i].astype(BF16)
    wdt = _pad_lanes(w_in[i][:, n_dt:], LANES).astype(BF16)
    mix_w = (_row(mix_norm[i]), win, wdt, conv_dw_w[i], _row(conv_dw_b[i]), _row(conv_ln_g[i]), _row(conv_ln_b[i]),
             ssd_conv_w[i], _row(ssd_conv_b[i]), _pad_lanes(_row(ssd_dt_bias[i]), LANES),
             _pad_lanes(_row(ssd_A_log[i]), LANES), _row(jnp.repeat(ssd_D[i], SSD_HEAD_DIM)),
             _row(ssd_norm[i]), head_expand, w_out[i].astype(BF16))
    f1_w = (_row(ffn1_norm[i]), ffn1_w_gate[i].astype(BF16), ffn1_w_up[i].astype(BF16), ffn1_w_down[i].astype(BF16))
    f2_w = (_row(ffn2_norm[i]), ffn2_w_gate[i].astype(BF16), ffn2_w_up[i].astype(BF16), ffn2_w_down[i].astype(BF16),
            _row(ple_norm[i]), ple_w_proj[i].astype(BF16), ple_w_gate[i].astype(BF16), _row(final_norm))
    zero_hist = (jnp.zeros((bp, CONV_WIDTH - 1, conv_ch), F32), jnp.zeros((bp, SSD_CONV_WIDTH - 1, xbc_dim), F32),
                 jnp.zeros((bp,) + state_ssd.shape[2:], F32))
    outs = []
    for x, p, hist in ((x_prompt, p_prompt[i], zero_hist),
                       (x_sample, p_sample[i], (state_conv[i], state_ssd_conv[i], state_ssd[i]))):
        b, t, _ = x.shape
        h2, ncb, nxb, nss = _ffn1_mixer(x, *hist, f1_w, mix_w)
        y = _ffn2(h2, p.reshape(b * t, -1), *f2_w).reshape(b, t, d)
        outs.append((y, ncb[None], nxb[None], nss[None]))
    (y_p, c1, x1, s1), (y_s, c2, x2, s2) = outs
    return (y_p, y_s, c1, x1, s1, c2, x2, s2)
```
